```python
import functools
import jax, jax.numpy as jnp
from jax import lax
import numpy as np

D_MODEL = 1024
BATCH = 8
SEQ = 4096
DEPTH = 1
DEC_BATCH = 32
DEC_SEQ = 4
PAST_LEN = 16384
PAGE_SIZE = 128

N_HEADS = 8
HEAD_DIM = 64
ATT_WIDTH = N_HEADS * HEAD_DIM
Q_BLOCK = 128
CONV_CH = D_MODEL // 2
CONV_WIDTH = 31
PEER_HEADS = 8
N_KEYS = 128
N_EXPERTS = N_KEYS * N_KEYS
PEER_KEY_DIM = 256
HALF_KEY = PEER_KEY_DIM // 2
TOPK = 16
PEER_BLOCK = 128
PLE_DIM = 256
RMS_EPS = 1e-6
LN_EPS = 1e-5
IN_SPLITS = (ATT_WIDTH, 2 * ATT_WIDTH, 3 * ATT_WIDTH, 3 * ATT_WIDTH + N_HEADS,
             3 * ATT_WIDTH + N_HEADS + CONV_CH, 3 * ATT_WIDTH + N_HEADS + 2 * CONV_CH,
             3 * ATT_WIDTH + N_HEADS + 2 * CONV_CH + D_MODEL)
IN_COLS = 3 * ATT_WIDTH + N_HEADS + 2 * CONV_CH + 2 * D_MODEL

kernel_name = 'fox_conformer_peer_hybrid_step'

F32 = jnp.float32


def rmsnorm(x, g):
    xf = x.astype(F32)
    y = xf * lax.rsqrt(jnp.mean(xf * xf, axis=-1, keepdims=True) + RMS_EPS)
    return (y * g.astype(F32)).astype(x.dtype)


def layernorm(x, g, b):
    xf = x.astype(F32)
    mu = jnp.mean(xf, axis=-1, keepdims=True)
    var = jnp.mean(jnp.square(xf - mu), axis=-1, keepdims=True)
    y = (xf - mu) * lax.rsqrt(var + LN_EPS)
    return (y * g.astype(F32) + b.astype(F32)).astype(x.dtype)


def in_projection(h, g_mix, w_in, b_f):
    u = rmsnorm(h, g_mix)
    z = u @ w_in
    q, k, v, fl, ca, cb, ga, gc = jnp.split(z, IN_SPLITS, axis=-1)
    lead = h.shape[:-1]
    q = q.reshape(*lead, N_HEADS, HEAD_DIM)
    k = k.reshape(*lead, N_HEADS, HEAD_DIM)
    v = v.reshape(*lead, N_HEADS, HEAD_DIM)
    logf = jax.nn.log_sigmoid(fl.astype(F32) + b_f.astype(F32))
    glu = ca * jax.nn.sigmoid(cb)
    return q, k, v, logf, glu, ga, gc


def fox_prompt(q, k, v, logf):
    B, S, H, Dh = q.shape
    scale = HEAD_DIM ** -0.5
    cum = jnp.cumsum(logf, axis=1)
    nb = S // Q_BLOCK
    qb = jnp.swapaxes(q.reshape(B, nb, Q_BLOCK, H, Dh), 0, 1)
    cqb = jnp.swapaxes(cum.reshape(B, nb, Q_BLOCK, H), 0, 1)
    qpos = jnp.arange(S, dtype=jnp.int32).reshape(nb, Q_BLOCK)
    kpos = jnp.arange(S, dtype=jnp.int32)
    cum_k = jnp.swapaxes(cum, 1, 2)

    def block(args):
        qi, ci, pi = args
        s = jnp.einsum('bqhd,bkhd->bhqk', qi, k).astype(F32) * scale
        s = s + jnp.swapaxes(ci, 1, 2)[..., None] - cum_k[:, :, None, :]
        s = jnp.where(kpos[None, None, None, :] <= pi[None, None, :, None], s, -jnp.inf)
        p = jax.nn.softmax(s, axis=-1).astype(v.dtype)
        return jnp.einsum('bhqk,bkhd->bqhd', p, v)

    o = lax.map(block, (qb, cqb, qpos))
    return jnp.swapaxes(o, 0, 1).reshape(B, S, H * Dh)


def fox_sample(q, k, v, logf, cache_k, cache_v, cache_lf, page_table):
    Bd, Sd, H, Dh = q.shape
    scale = HEAD_DIM ** -0.5
    L = page_table.shape[1] * cache_k.shape[1]
    kp = cache_k[page_table].reshape(Bd, L, H, Dh)
    vp = cache_v[page_table].reshape(Bd, L, H, Dh)
    lfp = cache_lf[page_table].reshape(Bd, L, H).astype(F32)
    excl = lax.cumsum(lfp, axis=1, reverse=True) - lfp
    cnew_t = jnp.swapaxes(jnp.cumsum(logf, axis=1), 1, 2)
    s_past = jnp.einsum('bqhd,bkhd->bhqk', q, kp).astype(F32) * scale
    s_past = s_past + jnp.swapaxes(excl, 1, 2)[:, :, None, :] + cnew_t[..., :, None]
    s_new = jnp.einsum('bqhd,bkhd->bhqk', q, k).astype(F32) * scale
    s_new = s_new + cnew_t[..., :, None] - cnew_t[..., None, :]
    causal = jnp.arange(Sd)[None, :] <= jnp.arange(Sd)[:, None]
    s_new = jnp.where(causal, s_new, -jnp.inf)
    p = jax.nn.softmax(jnp.concatenate([s_past, s_new], axis=-1), axis=-1).astype(v.dtype)
    o = (jnp.einsum('bhqk,bkhd->bqhd', p[..., :L], vp)
         + jnp.einsum('bhqk,bkhd->bqhd', p[..., L:], v))
    return o.reshape(Bd, Sd, H * Dh)


def conv_branch(glu_ext, w_dw, b_dw, ln_g, ln_b, w_conv_proj):
    y = lax.conv_general_dilated(
        glu_ext, w_dw[:, None, :].astype(glu_ext.dtype), window_strides=(1,), padding='VALID',
        dimension_numbers=('NWC', 'WIO', 'NWC'), feature_group_count=CONV_CH)
    y = jax.nn.silu(layernorm(y + b_dw, ln_g, ln_b))
    return y @ w_conv_proj


def peer(u, w_pq, sub_keys, exp_u, exp_v):
    lead = u.shape[:-1]
    x = u.reshape(-1, D_MODEL)
    T = x.shape[0]
    nb = -(-T // PEER_BLOCK)
    x = jnp.pad(x, ((0, nb * PEER_BLOCK - T), (0, 0))).reshape(nb, PEER_BLOCK, D_MODEL)

    def block(xb):
        q = (xb @ w_pq).reshape(PEER_BLOCK, PEER_HEADS, 2, HALF_KEY)
        s = jnp.einsum('thpd,pnd->thpn', q, sub_keys).astype(F32)
        sv, si = lax.top_k(s, TOPK)
        comb = (sv[:, :, 0, :, None] + sv[:, :, 1, None, :]).reshape(PEER_BLOCK, PEER_HEADS, TOPK * TOPK)
        cidx = (si[:, :, 0, :, None] * N_KEYS + si[:, :, 1, None, :]).reshape(PEER_BLOCK, PEER_HEADS, TOPK * TOPK)
        tv, ti = lax.top_k(comb, TOPK)
        eidx = jnp.take_along_axis(cidx, ti, axis=-1)
        g = jax.nn.softmax(tv, axis=-1)
        ue = exp_u[eidx]
        ve = exp_v[eidx]
        a = jax.nn.gelu(jnp.einsum('td,thkd->thk', xb, ue).astype(F32), approximate=False)
        return jnp.einsum('thk,thkd->td', (g * a).astype(ve.dtype), ve)

    y = lax.map(block, x)
    return y.reshape(-1, D_MODEL)[:T].reshape(*lead, D_MODEL)


def layer(h, p_i, conv_prev, attend, g_mix, w_in, b_f, w_dw, b_dw, ln_g, ln_b, w_conv_proj,
          w_att_proj, w_out, g_ffn, w_pq, sub_keys, exp_u, exp_v, g_ple, w_ple, w_ple_gate):
    q, k, v, logf, glu, ga, gc = in_projection(h, g_mix, w_in, b_f)
    att = attend(q, k, v, logf) @ w_att_proj
    glu_ext = jnp.concatenate([conv_prev.astype(glu.dtype), glu], axis=1)
    conv = conv_branch(glu_ext, w_dw, b_dw, ln_g, ln_b, w_conv_proj)
    merged = jax.nn.sigmoid(ga) * att + jax.nn.sigmoid(gc) * conv
    h = h + merged @ w_out
    h = h + peer(rmsnorm(h, g_ffn), w_pq, sub_keys, exp_u, exp_v)
    h = h + (p_i @ w_ple) * jax.nn.sigmoid(rmsnorm(h, g_ple) @ w_ple_gate)
    return h, k, v, logf, glu_ext[:, -(CONV_WIDTH - 1):]


def setup_inputs(seed: int = 0) -> dict:
    key = jax.random.key(seed)
    ks = jax.random.split(key, 32)
    n_pages = PAST_LEN // PAGE_SIZE
    n_pool = (DEC_BATCH * n_pages * 5 + 3) // 4
    nrm = lambda k, shape, s=1.0: jax.random.normal(k, shape, F32) * s
    page_table = jax.random.permutation(ks[0], n_pool)[:DEC_BATCH * n_pages].reshape(DEC_BATCH, n_pages).astype(jnp.int32)
    return {
        'x_prompt': nrm(ks[1], (BATCH, SEQ, D_MODEL)),
        'x_sample': nrm(ks[2], (DEC_BATCH, DEC_SEQ, D_MODEL)),
        'cache_k': nrm(ks[3], (DEPTH, n_pool, PAGE_SIZE, N_HEADS, HEAD_DIM)),
        'cache_v': nrm(ks[4], (DEPTH, n_pool, PAGE_SIZE, N_HEADS, HEAD_DIM)),
        'cache_lf': jax.nn.log_sigmoid(4.0 + nrm(ks[5], (DEPTH, n_pool, PAGE_SIZE, N_HEADS), 0.5)),
        'state_conv': nrm(ks[6], (DEPTH, DEC_BATCH, CONV_WIDTH - 1, CONV_CH)),
        'page_table': page_table,
        'p_prompt': nrm(ks[7], (DEPTH, BATCH, SEQ, PLE_DIM)),
        'p_sample': nrm(ks[8], (DEPTH, DEC_BATCH, DEC_SEQ, PLE_DIM)),
        'g_mix': 1.0 + nrm(ks[9], (DEPTH, D_MODEL), 0.05),
        'w_in': nrm(ks[10], (DEPTH, D_MODEL, IN_COLS), D_MODEL ** -0.5),
        'b_f': 4.0 + nrm(ks[11], (DEPTH, N_HEADS), 0.5),
        'w_dw': nrm(ks[12], (DEPTH, CONV_WIDTH, CONV_CH), CONV_WIDTH ** -0.5),
        'b_dw': nrm(ks[13], (DEPTH, CONV_CH), 0.02),
        'ln_g': 1.0 + nrm(ks[14], (DEPTH, CONV_CH), 0.05),
        'ln_b': nrm(ks[15], (DEPTH, CONV_CH), 0.02),
        'w_conv_proj': nrm(ks[16], (DEPTH, CONV_CH, D_MODEL), CONV_CH ** -0.5),
        'w_att_proj': nrm(ks[17], (DEPTH, ATT_WIDTH, D_MODEL), ATT_WIDTH ** -0.5),
        'w_out': nrm(ks[18], (DEPTH, D_MODEL, D_MODEL), D_MODEL ** -0.5),
        'g_ffn': 1.0 + nrm(ks[19], (DEPTH, D_MODEL), 0.05),
        'w_pq': nrm(ks[20], (DEPTH, D_MODEL, PEER_HEADS * PEER_KEY_DIM), D_MODEL ** -0.5),
        'sub_keys': nrm(ks[21], (DEPTH, 2, N_KEYS, HALF_KEY), HALF_KEY ** -0.5),
        'exp_u': nrm(ks[22], (DEPTH, N_EXPERTS, D_MODEL), D_MODEL ** -0.5),
        'exp_v': nrm(ks[23], (DEPTH, N_EXPERTS, D_MODEL), PEER_HEADS ** -0.5),
        'g_ple': 1.0 + nrm(ks[24], (DEPTH, D_MODEL), 0.05),
        'w_ple': nrm(ks[25], (DEPTH, PLE_DIM, D_MODEL), PLE_DIM ** -0.5),
        'w_ple_gate': nrm(ks[26], (DEPTH, D_MODEL, D_MODEL), D_MODEL ** -0.5),
        'g_final': 1.0 + nrm(ks[27], (D_MODEL,), 0.05),
    }


def reference(x_prompt, x_sample, cache_k, cache_v, cache_lf, state_conv, page_table,
              p_prompt, p_sample, g_mix, w_in, b_f, w_dw, b_dw, ln_g, ln_b, w_conv_proj,
              w_att_proj, w_out, g_ffn, w_pq, sub_keys, exp_u, exp_v, g_ple, w_ple,
              w_ple_gate, g_final):
    hp, hs = x_prompt, x_sample
    kps, vps, lfps, cps = [], [], [], []
    kss, vss, lfss, css = [], [], [], []
    for i in range(DEPTH):
        lw = (g_mix[i], w_in[i], b_f[i], w_dw[i], b_dw[i], ln_g[i], ln_b[i], w_conv_proj[i],
              w_att_proj[i], w_out[i], g_ffn[i], w_pq[i], sub_keys[i], exp_u[i], exp_v[i],
              g_ple[i], w_ple[i], w_ple_gate[i])
        conv0 = jnp.zeros((hp.shape[0], CONV_WIDTH - 1, CONV_CH), hp.dtype)
        hp, kp, vp, lfp, cp = layer(hp, p_prompt[i], conv0, fox_prompt, *lw)
        kps.append(kp); vps.append(vp); lfps.append(lfp); cps.append(cp)
        att_s = functools.partial(fox_sample, cache_k=cache_k[i], cache_v=cache_v[i],
                                  cache_lf=cache_lf[i], page_table=page_table)
        hs, ks_, vs_, lfs_, cs_ = layer(hs, p_sample[i], state_conv[i], att_s, *lw)
        kss.append(ks_); vss.append(vs_); lfss.append(lfs_); css.append(cs_)
    y_prompt = rmsnorm(hp, g_final)
    y_sample = rmsnorm(hs, g_final)
    return (y_prompt, y_sample,
            jnp.stack(kps), jnp.stack(vps), jnp.stack(lfps), jnp.stack(cps),
            jnp.stack(kss), jnp.stack(vss), jnp.stack(lfss), jnp.stack(css))
```

```python
import functools

import jax
import jax.numpy as jnp
from jax import lax
from jax.experimental import pallas as pl
from jax.experimental.pallas import tpu as pltpu

F32 = jnp.float32
BF16 = jnp.bfloat16
I32 = jnp.int32

D_MODEL = 1024
N_HEADS = 8
HEAD_DIM = 64
ATT_WIDTH = N_HEADS * HEAD_DIM
CONV_CH = D_MODEL // 2
CONV_WIDTH = 31
PEER_HEADS = 8
N_KEYS = 128
HALF_KEY = 128
TOPK = 16
N_EXPERTS = N_KEYS * N_KEYS
RMS_EPS = 1e-6
LN_EPS = 1e-5

V7X_LANES = 128
V7X_SUBLANES = 8
V7X_VMEM_LIMIT_BYTES = 56 * 1024 * 1024

NEG_INF = float("-inf")
HIGHEST = lax.Precision.HIGHEST
NT_DIMS = (((1,), (1,)), ((), ()))


def _cparams(*sem):
    return pltpu.CompilerParams(dimension_semantics=sem, vmem_limit_bytes=V7X_VMEM_LIMIT_BYTES)


def _const_spec(shape):
    nd = len(shape)
    return pl.BlockSpec(shape, lambda *_: (0,) * nd, pipeline_mode=pl.Buffered(1))


def _rms(x, g):
    return x * lax.rsqrt(jnp.mean(x * x, axis=-1, keepdims=True) + RMS_EPS) * g


def _log_sigmoid(z):
    return jnp.minimum(z, 0.0) - jnp.log1p(jnp.exp(-jnp.abs(z)))


def _inproj_kernel(x_ref, g_ref, wqkv_ref, wf_ref, bf_ref, wglu_ref,
                   q_ref, k_ref, v_ref, kb_ref, vb_ref, lf_ref, cum_ref, glu_ref, carry_ref,
                   *, tm, seg):
    i = pl.program_id(0)
    u = _rms(x_ref[...], g_ref[...]).astype(BF16)
    qkv = jnp.dot(u, wqkv_ref[...], preferred_element_type=F32)
    q_ref[...] = (qkv[:, :ATT_WIDTH] * (HEAD_DIM ** -0.5)).astype(BF16)
    k = qkv[:, ATT_WIDTH:2 * ATT_WIDTH]
    v = qkv[:, 2 * ATT_WIDTH:]
    k_ref[...] = k
    v_ref[...] = v
    kb_ref[...] = k.astype(BF16)
    vb_ref[...] = v.astype(BF16)
    cab = jnp.dot(u, wglu_ref[...], preferred_element_type=F32)
    glu_ref[...] = cab[:, :CONV_CH] * jax.nn.sigmoid(cab[:, CONV_CH:])
    fl = lax.dot_general(wf_ref[...], u, NT_DIMS, preferred_element_type=F32)
    lf = _log_sigmoid(fl + bf_ref[...])
    lf_ref[...] = lf
    src = lax.broadcasted_iota(I32, (tm, tm), 0)
    dst = lax.broadcasted_iota(I32, (tm, tm), 1)
    keep = src <= dst
    if seg < tm:
        keep = keep & ((src // seg) == (dst // seg))
    tri = jnp.where(keep, 1.0, 0.0).astype(F32)
    cum = jnp.dot(lf, tri, preferred_element_type=F32, precision=HIGHEST)
    if seg > tm:
        @pl.when(i % (seg // tm) == 0)
        def _():
            carry_ref[...] = jnp.zeros_like(carry_ref)
        cum = cum + carry_ref[:, 0:1]
        carry_ref[...] = jnp.broadcast_to(cum[:, tm - 1:tm], carry_ref.shape)
    cum_ref[...] = cum


def _in_projection(x, g_mix, wqkv, wf_t, b_f, wglu, *, seg, tm):
    t = x.shape[0]
    row = lambda w: pl.BlockSpec((tm, w), lambda i: (i, 0))
    col = pl.BlockSpec((N_HEADS, tm), lambda i: (0, i))
    out_shape = (
        jax.ShapeDtypeStruct((t, ATT_WIDTH), BF16),
        jax.ShapeDtypeStruct((t, ATT_WIDTH), F32),
        jax.ShapeDtypeStruct((t, ATT_WIDTH), F32),
        jax.ShapeDtypeStruct((t, ATT_WIDTH), BF16),
        jax.ShapeDtypeStruct((t, ATT_WIDTH), BF16),
        jax.ShapeDtypeStruct((N_HEADS, t), F32),
        jax.ShapeDtypeStruct((N_HEADS, t), F32),
        jax.ShapeDtypeStruct((t, CONV_CH), F32),
    )
    return pl.pallas_call(
        functools.partial(_inproj_kernel, tm=tm, seg=seg),
        grid=(t // tm,),
        in_specs=[row(D_MODEL), _const_spec((1, D_MODEL)), _const_spec(wqkv.shape),
                  _const_spec(wf_t.shape), _const_spec((N_HEADS, 1)), _const_spec(wglu.shape)],
        out_specs=(row(ATT_WIDTH), row(ATT_WIDTH), row(ATT_WIDTH), row(ATT_WIDTH), row(ATT_WIDTH),
                   col, col, row(CONV_CH)),
        out_shape=out_shape,
        scratch_shapes=[pltpu.VMEM((N_HEADS, V7X_LANES), F32)],
        compiler_params=_cparams("arbitrary"),
        name="in_projection",
    )(x, g_mix, wqkv, wf_t, b_f, wglu)


def _fox_prompt_kernel(q_ref, k_ref, v_ref, cum_ref, o_ref, m_ref, l_ref, acc_ref, *, tq):
    qi = pl.program_id(2)
    q = q_ref[...]
    lane = lax.broadcasted_iota(I32, q.shape, 1)
    first_head = lane < HEAD_DIM
    zero = jnp.zeros_like(q)
    q_heads = (jnp.where(first_head, q, zero), jnp.where(first_head, zero, q))
    m_ref[...] = jnp.full(m_ref.shape, NEG_INF, F32)
    l_ref[...] = jnp.zeros(l_ref.shape, F32)
    acc_ref[...] = jnp.zeros(acc_ref.shape, F32)

    def tile(kj, diagonal):
        ks = pl.multiple_of(kj * tq, tq)
        kt = k_ref[pl.ds(ks, tq), :]
        vt = v_ref[pl.ds(ks, tq), :]
        for h in range(2):
            s = lax.dot_general(q_heads[h], kt, NT_DIMS, preferred_element_type=F32)
            s = s - cum_ref[h:h + 1, pl.ds(ks, tq)]
            if diagonal:
                r = lax.broadcasted_iota(I32, s.shape, 0)
                c = lax.broadcasted_iota(I32, s.shape, 1)
                s = jnp.where(c <= r, s, NEG_INF)
            m_old = m_ref[h]
            m_new = jnp.maximum(m_old, jnp.max(s, axis=1, keepdims=True))
            alpha = jnp.exp(m_old - m_new)
            p = jnp.exp(s - m_new)
            l_ref[h] = alpha * l_ref[h] + jnp.sum(p, axis=1, keepdims=True)
            acc_ref[h] = alpha * acc_ref[h] + jnp.dot(p.astype(BF16), vt, preferred_element_type=F32)
            m_ref[h] = m_new

    def body(kj, carry):
        tile(kj, False)
        return carry

    lax.fori_loop(0, qi, body, 0)
    tile(qi, True)
    o0 = acc_ref[0] / l_ref[0]
    o1 = acc_ref[1] / l_ref[1]
    o_ref[...] = jnp.where(first_head, o0, o1).astype(o_ref.dtype)


def _fox_prompt(q, kb, vb, cum_pairs, *, batch, seq, tq):
    nq = seq // tq
    pairs = N_HEADS // 2
    return pl.pallas_call(
        functools.partial(_fox_prompt_kernel, tq=tq),
        grid=(batch, pairs, nq),
        in_specs=[
            pl.BlockSpec((tq, V7X_LANES), lambda b, p, i: (b * nq + i, p)),
            pl.BlockSpec((seq, V7X_LANES), lambda b, p, i: (b, p)),
            pl.BlockSpec((seq, V7X_LANES), lambda b, p, i: (b, p)),
            pl.BlockSpec((None, 2, seq), lambda b, p, i: (p, 0, b)),
        ],
        out_specs=pl.BlockSpec((tq, V7X_LANES), lambda b, p, i: (b * nq + i, p)),
        out_shape=jax.ShapeDtypeStruct(q.shape, BF16),
        scratch_shapes=[pltpu.VMEM((2, tq, 1), F32), pltpu.VMEM((2, tq, 1), F32),
                        pltpu.VMEM((2, tq, V7X_LANES), F32)],
        compiler_params=_cparams("arbitrary", "arbitrary", "arbitrary"),
        name="fox_prompt",
    )(q, kb, vb, cum_pairs)


def _fox_sample_kernel(pt_ref, q_ref, kn_ref, vn_ref, cn_ref, kp_ref, vp_ref, lfp_ref, o_ref,
                       qbd_ref, m_ref, l_ref, acc_ref, suf_ref, *, n_pages, n_new, page):
    del pt_ref
    j = pl.program_id(1)
    rows = n_new * N_HEADS
    head_of_lane = lax.broadcasted_iota(I32, (N_HEADS, ATT_WIDTH), 1) // HEAD_DIM
    own_head = head_of_lane == lax.broadcasted_iota(I32, (N_HEADS, ATT_WIDTH), 0)

    def update(kt, vt, bias):
        s = lax.dot_general(qbd_ref[...], kt, NT_DIMS, preferred_element_type=F32) + bias
        m_old = m_ref[...]
        m_new = jnp.maximum(m_old, jnp.max(s, axis=1, keepdims=True))
        alpha = jnp.exp(m_old - m_new)
        p = jnp.exp(s - m_new)
        l_ref[...] = alpha * l_ref[...] + jnp.sum(p, axis=1, keepdims=True)
        acc_ref[...] = alpha * acc_ref[...] + jnp.dot(p.astype(BF16), vt, preferred_element_type=F32)
        m_ref[...] = m_new

    @pl.when(j == 0)
    def _():
        q = q_ref[...]
        qbd = jnp.concatenate(
            [jnp.where(own_head, jnp.broadcast_to(q[t:t + 1, :], own_head.shape), 0.0) for t in range(n_new)],
            axis=0)
        qbd_ref[...] = qbd.astype(BF16)
        m_ref[...] = jnp.full(m_ref.shape, NEG_INF, F32)
        l_ref[...] = jnp.zeros(l_ref.shape, F32)
        acc_ref[...] = jnp.zeros(acc_ref.shape, F32)
        suf_ref[...] = jnp.zeros(suf_ref.shape, F32)
        cn = jnp.concatenate([cn_ref[...]] * n_new, axis=0)
        tok = lax.broadcasted_iota(I32, (rows, page), 0) // N_HEADS
        col = lax.broadcasted_iota(I32, (rows, page), 1)
        bias = jnp.where(col <= tok, -cn, NEG_INF)
        update(kn_ref[...], vn_ref[...], bias)

    @pl.when(j > 0)
    def _():
        src = lax.broadcasted_iota(I32, (page, 2 * page), 0)
        dst = lax.broadcasted_iota(I32, (page, 2 * page), 1)
        sel = jnp.where((src > dst) | (dst >= page), 1.0, 0.0).astype(F32)
        sums = jnp.dot(lfp_ref[...], sel, preferred_element_type=F32, precision=HIGHEST)
        excl = sums[:, :page] + suf_ref[...]
        suf_ref[...] = suf_ref[...] + sums[:, page:]
        bias = jnp.concatenate([excl] * n_new, axis=0)
        update(kp_ref[...].astype(BF16), vp_ref[...].astype(BF16), bias)

    @pl.when(j == n_pages)
    def _():
        o = acc_ref[...] / l_ref[...]
        out = [jnp.sum(jnp.where(own_head, o[t * N_HEADS:(t + 1) * N_HEADS, :], 0.0), axis=0, keepdims=True)
               for t in range(n_new)]
        o_ref[...] = jnp.concatenate(out, axis=0)


def _fox_sample(q, kn, vn, cn, cache_k, cache_v, cache_lf_t, page_table):
    bd, n_new, _ = q.shape
    n_pages = page_table.shape[1]
    page = cache_k.shape[1]
    rows = n_new * N_HEADS
    pool_idx = lambda b, j, pt: (pt[b, n_pages - jnp.maximum(j, 1)], 0, 0)
    per_seq = lambda b, j, pt: (b, 0, 0)
    grid_spec = pltpu.PrefetchScalarGridSpec(
        num_scalar_prefetch=1,
        grid=(bd, n_pages + 1),
        in_specs=[
            pl.BlockSpec((None, n_new, ATT_WIDTH), per_seq),
            pl.BlockSpec((None, page, ATT_WIDTH), per_seq),
            pl.BlockSpec((None, page, ATT_WIDTH), per_seq),
            pl.BlockSpec((None, N_HEADS, page), per_seq),
            pl.BlockSpec((None, page, ATT_WIDTH), pool_idx),
            pl.BlockSpec((None, page, ATT_WIDTH), pool_idx),
            pl.BlockSpec((None, N_HEADS, page), pool_idx),
        ],
        out_specs=pl.BlockSpec((None, n_new, ATT_WIDTH), per_seq),
        scratch_shapes=[pltpu.VMEM((rows, ATT_WIDTH), BF16), pltpu.VMEM((rows, 1), F32),
                        pltpu.VMEM((rows, 1), F32), pltpu.VMEM((rows, ATT_WIDTH), F32),
                        pltpu.VMEM((N_HEADS, page), F32)],
    )
    return pl.pallas_call(
        functools.partial(_fox_sample_kernel, n_pages=n_pages, n_new=n_new, page=page),
        grid_spec=grid_spec,
        out_shape=jax.ShapeDtypeStruct((bd, n_new, ATT_WIDTH), F32),
        compiler_params=_cparams("arbitrary", "arbitrary"),
        name="fox_sample",
    )(page_table, q, kn, vn, cn, cache_k, cache_v, cache_lf_t)


HALO = 32


def _dwconv_prompt_kernel(glu_ref, halo_ref, w_ref, b_ref, y_ref, cat_ref, *, tm, tiles_per_seq):
    i = pl.program_id(0)
    starts_sequence = (i % tiles_per_seq) == 0
    cat_ref[0:HALO, :] = jnp.where(starts_sequence, 0.0, halo_ref[...])
    cat_ref[HALO:HALO + tm, :] = glu_ref[...]
    off = HALO - (CONV_WIDTH - 1)
    acc = jnp.broadcast_to(b_ref[...], (tm, CONV_CH))
    for w in range(CONV_WIDTH):
        acc = acc + cat_ref[pl.ds(off + w, tm), :] * w_ref[w:w + 1, :]
    y_ref[...] = acc


def _dwconv_prompt(glu, w_dw, b_dw, *, seq, tm):
    t = glu.shape[0]
    per_tile = tm // HALO
    return pl.pallas_call(
        functools.partial(_dwconv_prompt_kernel, tm=tm, tiles_per_seq=seq // tm),
        grid=(t // tm,),
        in_specs=[pl.BlockSpec((tm, CONV_CH), lambda i: (i, 0)),
                  pl.BlockSpec((HALO, CONV_CH), lambda i: (jnp.maximum(i * per_tile - 1, 0), 0)),
                  _const_spec(w_dw.shape), _const_spec(b_dw.shape)],
        out_specs=pl.BlockSpec((tm, CONV_CH), lambda i: (i, 0)),
        out_shape=jax.ShapeDtypeStruct(glu.shape, F32),
        scratch_shapes=[pltpu.VMEM((HALO + tm, CONV_CH), F32)],
        compiler_params=_cparams("arbitrary"),
        name="dwconv_prompt",
    )(glu, glu, w_dw, b_dw)


def _dwconv_sample_kernel(hist_ref, new_ref, w_ref, b_ref, y_ref, *, n_new):
    hist = CONV_WIDTH - 1
    for t in range(n_new):
        acc = jnp.broadcast_to(b_ref[...], y_ref.shape[1:])
        for w in range(CONV_WIDTH):
            pos = t + w
            row = hist_ref[pos] if pos < hist else new_ref[pos - hist]
            acc = acc + row * w_ref[w:w + 1, :]
        y_ref[t] = acc


def _dwconv_sample(hist_tm, new_tm, w_dw, b_dw):
    n_new = new_tm.shape[0]
    return pl.pallas_call(
        functools.partial(_dwconv_sample_kernel, n_new=n_new),
        out_shape=jax.ShapeDtypeStruct(new_tm.shape, F32),
        name="dwconv_sample",
    )(hist_tm, new_tm, w_dw, b_dw)


def _mix_kernel(x_ref, o_ref, y_ref, gmix_ref, wgate_ref, watt_ref, lng_ref, lnb_ref, wconv_ref,
                wout_ref, gffn_ref, wpq_ref, h_ref, xb_ref, qp_ref):
    x = x_ref[...]
    y = y_ref[...]
    mu = jnp.mean(y, axis=-1, keepdims=True)
    var = jnp.mean(jnp.square(y - mu), axis=-1, keepdims=True)
    z = (y - mu) * lax.rsqrt(var + LN_EPS) * lng_ref[...] + lnb_ref[...]
    z = z * jax.nn.sigmoid(z)
    conv = jnp.dot(z.astype(BF16), wconv_ref[...], preferred_element_type=F32)
    att = jnp.dot(o_ref[...].astype(BF16), watt_ref[...], preferred_element_type=F32)
    u = _rms(x, gmix_ref[...]).astype(BF16)
    gates = jnp.dot(u, wgate_ref[...], preferred_element_type=F32)
    merged = jax.nn.sigmoid(gates[:, :D_MODEL]) * att + jax.nn.sigmoid(gates[:, D_MODEL:]) * conv
    h = x + jnp.dot(merged.astype(BF16), wout_ref[...], preferred_element_type=F32)
    h_ref[...] = h
    xb = _rms(h, gffn_ref[...])
    xb_ref[...] = xb
    qp_ref[...] = jnp.dot(xb.astype(BF16), wpq_ref[...], preferred_element_type=F32)


def _mix(x, o, ydw, g_mix, wgate, watt, ln_g, ln_b, wconv, wout, g_ffn, wpq, *, tm):
    t = x.shape[0]
    row = lambda w: pl.BlockSpec((tm, w), lambda i: (i, 0))
    nq = wpq.shape[1]
    return pl.pallas_call(
        _mix_kernel,
        grid=(t // tm,),
        in_specs=[row(D_MODEL), row(ATT_WIDTH), row(CONV_CH), _const_spec(g_mix.shape),
                  _const_spec(wgate.shape), _const_spec(watt.shape), _const_spec(ln_g.shape),
                  _const_spec(ln_b.shape), _const_spec(wconv.shape), _const_spec(wout.shape),
                  _const_spec(g_ffn.shape), _const_spec(wpq.shape)],
        out_specs=(row(D_MODEL), row(D_MODEL), row(nq)),
        out_shape=(jax.ShapeDtypeStruct((t, D_MODEL), F32), jax.ShapeDtypeStruct((t, D_MODEL), F32),
                   jax.ShapeDtypeStruct((t, nq), F32)),
        compiler_params=_cparams("arbitrary"),
        name="mix",
    )(x, o, ydw, g_mix, wgate, watt, ln_g, ln_b, wconv, wout, g_ffn, wpq)


def _topk_rows(x, k, payload=None):
    n, t = x.shape
    row = lax.broadcasted_iota(I32, (n, t), 0)
    slot = lax.broadcasted_iota(I32, (k, t), 0)

    def step(i, carry):
        x, vals, picks = carry
        m = jnp.max(x, axis=0, keepdims=True)
        first = jnp.min(jnp.where(x == m, row, n), axis=0, keepdims=True)
        hit = row == first
        pick = first if payload is None else jnp.max(jnp.where(hit, payload, -1), axis=0, keepdims=True)
        vals = jnp.where(slot == i, m, vals)
        picks = jnp.where(slot == i, pick, picks)
        return jnp.where(hit, NEG_INF, x), vals, picks

    _, vals, picks = lax.fori_loop(0, k, step, (x, jnp.zeros((k, t), F32), jnp.zeros((k, t), I32)))
    return vals, picks


def _peer_route_kernel(qp_ref, keys_ref, e_ref, g_ref):
    def head(h, carry):
        sv, si = [], []
        for p in range(2):
            c0 = pl.multiple_of((h * 2 + p) * HALF_KEY, HALF_KEY)
            q_hp = qp_ref[:, pl.ds(c0, HALF_KEY)].astype(BF16)
            s = lax.dot_general(keys_ref[p], q_hp, NT_DIMS, preferred_element_type=F32)
            v, i = _topk_rows(s, TOPK)
            sv.append(v)
            si.append(i)
        comb = jnp.concatenate([sv[0][i:i + 1, :] + sv[1] for i in range(TOPK)], axis=0)
        cidx = jnp.concatenate([si[0][i:i + 1, :] * N_KEYS + si[1] for i in range(TOPK)], axis=0)
        tv, eidx = _topk_rows(comb, TOPK, payload=cidx)
        ex = jnp.exp(tv - jnp.max(tv, axis=0, keepdims=True))
        r0 = pl.multiple_of(h * TOPK, TOPK)
        e_ref[pl.ds(r0, TOPK), :] = eidx
        g_ref[pl.ds(r0, TOPK), :] = ex / jnp.sum(ex, axis=0, keepdims=True)
        return carry

    lax.fori_loop(0, PEER_HEADS, head, 0)


def _peer_route(qp, keys, *, tm):
    t = qp.shape[0]
    rows = PEER_HEADS * TOPK
    out = pl.BlockSpec((rows, tm), lambda i: (0, i))
    return pl.pallas_call(
        _peer_route_kernel,
        grid=(t // tm,),
        in_specs=[pl.BlockSpec((tm, qp.shape[1]), lambda i: (i, 0)), _const_spec(keys.shape)],
        out_specs=(out, out),
        out_shape=(jax.ShapeDtypeStruct((rows, t), I32), jax.ShapeDtypeStruct((rows, t), F32)),
        compiler_params=_cparams("arbitrary"),
        name="peer_route",
    )(qp, keys)


HALF_EXPERTS = N_EXPERTS // 2
PAIRS = PEER_HEADS * TOPK
TILE_ROWS = D_MODEL // V7X_LANES


def _expert_tile(tab_ref, e):
    both = tab_ref[e & (HALF_EXPERTS - 1)].astype(F32)
    return jnp.where(e >= HALF_EXPERTS, both[TILE_ROWS:], both[:TILE_ROWS])


def _bitrev3(j):
    return ((j & 1) << 2) | (j & 2) | ((j >> 2) & 1)


def _sublane_sums(ps):
    sub = lax.broadcasted_iota(I32, ps[0].shape, 0)
    level = [ps[_bitrev3(q)] for q in range(len(ps))]
    for h in (4, 2, 1):
        first = (sub & (2 * h - 1)) < h
        level = [jnp.where(first, x, pltpu.roll(y, h, 0)) + jnp.where(first, pltpu.roll(x, V7X_SUBLANES - h, 0), y)
                 for x, y in zip(level[0::2], level[1::2])]
    return level[0]


def _peer_dot_kernel(idx_ref, x_ref, g_ref, tab_ref, w_ref, r_ref, a_ref, *, tb):
    ones = jnp.ones((V7X_SUBLANES, V7X_LANES), BF16)

    def token(t, carry):
        xt = x_ref[t]
        for grp in range(PAIRS // V7X_SUBLANES):
            ps = [_expert_tile(tab_ref, idx_ref[t, grp * V7X_SUBLANES + j]) * xt for j in range(V7X_SUBLANES)]
            r_ref[grp * V7X_SUBLANES:(grp + 1) * V7X_SUBLANES, :] = _sublane_sums(ps)
        r = r_ref[...]
        hi = r.astype(BF16)
        lo = (r - hi.astype(F32)).astype(BF16)
        a_row = (lax.dot_general(ones, hi, NT_DIMS, preferred_element_type=F32)
                 + lax.dot_general(ones, lo, NT_DIMS, preferred_element_type=F32))
        a_ref[pl.ds(t, 1), :] = a_row[0:1, :]
        return carry

    lax.fori_loop(0, tb, token, 0)
    a = a_ref[...]
    gelu = a * (lax.erf(a * (2.0 ** -0.5)) + 1.0) * 0.5
    w_ref[...] = g_ref[...] * gelu


def _peer_dot(idx, x_tiles, g, tab, *, tb):
    t = idx.shape[0]
    return pl.pallas_call(
        functools.partial(_peer_dot_kernel, tb=tb),
        grid=(t // tb,),
        in_specs=[pl.BlockSpec((tb, PAIRS), lambda i: (i, 0), memory_space=pltpu.SMEM),
                  pl.BlockSpec((tb, TILE_ROWS, V7X_LANES), lambda i: (i, 0, 0)),
                  pl.BlockSpec((tb, PAIRS), lambda i: (i, 0)),
                  _const_spec(tab.shape)],
        out_specs=pl.BlockSpec((tb, PAIRS), lambda i: (i, 0)),
        out_shape=jax.ShapeDtypeStruct((t, PAIRS), F32),
        scratch_shapes=[pltpu.VMEM((PAIRS, V7X_LANES), F32), pltpu.VMEM((tb, PAIRS), F32)],
        compiler_params=_cparams("arbitrary"),
        name="peer_dot",
    )(idx, x_tiles, g, tab)


def _peer_sum_kernel(idx_ref, w_ref, tab_ref, y_ref, *, tb):
    n_acc = 4

    def token(t, carry):
        acc = [jnp.zeros((TILE_ROWS, V7X_LANES), F32)] * n_acc
        for k in range(PAIRS):
            acc[k % n_acc] = acc[k % n_acc] + w_ref[t, k] * _expert_tile(tab_ref, idx_ref[t, k])
        y_ref[t] = (acc[0] + acc[1]) + (acc[2] + acc[3])
        return carry

    lax.fori_loop(0, tb, token, 0)


def _peer_sum(idx, w, tab, *, tb):
    t = idx.shape[0]
    smem = pl.BlockSpec((tb, PAIRS), lambda i: (i, 0), memory_space=pltpu.SMEM)
    return pl.pallas_call(
        functools.partial(_peer_sum_kernel, tb=tb),
        grid=(t // tb,),
        in_specs=[smem, smem, _const_spec(tab.shape)],
        out_specs=pl.BlockSpec((tb, TILE_ROWS, V7X_LANES), lambda i: (i, 0, 0)),
        out_shape=jax.ShapeDtypeStruct((t, TILE_ROWS, V7X_LANES), F32),
        compiler_params=_cparams("arbitrary"),
        name="peer_sum",
    )(idx, w, tab)


def _ple_kernel(h_ref, y_ref, p_ref, gple_ref, wple_ref, wgate_ref, gfin_ref, o_ref):
    h = h_ref[...] + y_ref[...]
    gate = jax.nn.sigmoid(jnp.dot(_rms(h, gple_ref[...]).astype(BF16), wgate_ref[...], preferred_element_type=F32))
    h = h + jnp.dot(p_ref[...].astype(BF16), wple_ref[...], preferred_element_type=F32) * gate
    o_ref[...] = _rms(h, gfin_ref[...])


def _ple(h, y, p, g_ple, wple, wgate, g_final, *, tm):
    t = h.shape[0]
    row = lambda w: pl.BlockSpec((tm, w), lambda i: (i, 0))
    return pl.pallas_call(
        _ple_kernel,
        grid=(t // tm,),
        in_specs=[row(D_MODEL), row(D_MODEL), row(p.shape[1]), _const_spec(g_ple.shape),
                  _const_spec(wple.shape), _const_spec(wgate.shape), _const_spec(g_final.shape)],
        out_specs=row(D_MODEL),
        out_shape=jax.ShapeDtypeStruct((t, D_MODEL), F32),
        compiler_params=_cparams("arbitrary"),
        name="ple_out",
    )(h, y, p, g_ple, wple, wgate, g_final)


def _pack_expert_table(tab):
    t = tab.astype(BF16).reshape(2, HALF_EXPERTS, TILE_ROWS, V7X_LANES)
    return jnp.swapaxes(t, 0, 1).reshape(HALF_EXPERTS, 2 * TILE_ROWS, V7X_LANES)


def _trunk(x, o, ydw, p, wts, *, tm, tb):
    h, xb, qp = _mix(x, o, ydw, wts["g_mix"], wts["wgate"], wts["watt"], wts["ln_g"], wts["ln_b"],
                     wts["wconv"], wts["wout"], wts["g_ffn"], wts["wpq"], tm=tm)
    e_t, g_t = _peer_route(qp, wts["keys"], tm=V7X_LANES)
    idx = e_t.T
    t = x.shape[0]
    w = _peer_dot(idx, xb.reshape(t, TILE_ROWS, V7X_LANES), g_t.T, wts["tab_u"], tb=tb)
    y = _peer_sum(idx, w, wts["tab_v"], tb=tb).reshape(t, D_MODEL)
    return _ple(h, y, p, wts["g_ple"], wts["wple"], wts["wplegate"], wts["g_final"], tm=tm)


def kernel(x_prompt, x_sample, cache_k, cache_v, cache_lf, state_conv, page_table, p_prompt, p_sample,
           g_mix, w_in, b_f, w_dw, b_dw, ln_g, ln_b, w_conv_proj, w_att_proj, w_out, g_ffn, w_pq,
           sub_keys, exp_u, exp_v, g_ple, w_ple, w_ple_gate, g_final):
    assert g_mix.shape[0] == 1, "single-layer trunk"
    b, s, _ = x_prompt.shape
    bd, sd, _ = x_sample.shape
    tp, ts = b * s, bd * sd
    page = cache_k.shape[2]

    w = w_in[0]
    c_f, c_glu, c_gate = 3 * ATT_WIDTH, 3 * ATT_WIDTH + N_HEADS, 3 * ATT_WIDTH + N_HEADS + 2 * CONV_CH
    wqkv = w[:, :c_f].astype(BF16)
    wf_t = w[:, c_f:c_glu].T.astype(BF16)
    wglu = w[:, c_glu:c_gate].astype(BF16)
    row = lambda a: a.reshape(1, -1)
    w_dw_pad = jnp.pad(w_dw[0], ((0, 1), (0, 0)))
    wts = dict(
        g_mix=row(g_mix[0]), wgate=w[:, c_gate:].astype(BF16), watt=w_att_proj[0].astype(BF16),
        ln_g=row(ln_g[0]), ln_b=row(ln_b[0]), wconv=w_conv_proj[0].astype(BF16), wout=w_out[0].astype(BF16),
        g_ffn=row(g_ffn[0]), wpq=w_pq[0].astype(BF16), keys=sub_keys[0].astype(BF16),
        tab_u=_pack_expert_table(exp_u[0]), tab_v=_pack_expert_table(exp_v[0]),
        g_ple=row(g_ple[0]), wple=w_ple[0].astype(BF16), wplegate=w_ple_gate[0].astype(BF16),
        g_final=row(g_final))
    b_f_col = b_f[0].reshape(N_HEADS, 1)
    b_dw_row = row(b_dw[0])

    xp = x_prompt.reshape(tp, D_MODEL)
    q, k, v, kb, vb, lf, cum, glu = _in_projection(xp, wts["g_mix"], wqkv, wf_t, b_f_col, wglu, seg=s, tm=512)
    o = _fox_prompt(q, kb, vb, cum.reshape(N_HEADS // 2, 2, tp), batch=b, seq=s, tq=512)
    ydw = _dwconv_prompt(glu, w_dw_pad, b_dw_row, seq=s, tm=512)
    y_prompt = _trunk(xp, o, ydw, p_prompt[0].reshape(tp, -1), wts, tm=256, tb=64).reshape(b, s, D_MODEL)
    k_prompt = k.reshape(1, b, s, N_HEADS, HEAD_DIM)
    v_prompt = v.reshape(1, b, s, N_HEADS, HEAD_DIM)
    lf_prompt = lf.T.reshape(1, b, s, N_HEADS)
    conv_prompt = glu.reshape(b, s, CONV_CH)[None, :, s - (CONV_WIDTH - 1):, :]

    xs = x_sample.reshape(ts, D_MODEL)
    q, k, v, kb, vb, lf, cum, glu = _in_projection(xs, wts["g_mix"], wqkv, wf_t, b_f_col, wglu, seg=sd, tm=ts)
    pad_page = lambda a: jnp.pad(a.reshape(bd, sd, ATT_WIDTH), ((0, 0), (0, page - sd), (0, 0)))
    cn = jnp.pad(cum.reshape(N_HEADS, bd, sd).transpose(1, 0, 2), ((0, 0), (0, 0), (0, page - sd)))
    n_pool = cache_k.shape[1]
    o = _fox_sample(q.astype(F32).reshape(bd, sd, ATT_WIDTH), pad_page(kb), pad_page(vb), cn,
                    cache_k[0].reshape(n_pool, page, ATT_WIDTH), cache_v[0].reshape(n_pool, page, ATT_WIDTH),
                    jnp.swapaxes(cache_lf[0], 1, 2), page_table).reshape(ts, ATT_WIDTH)
    glu_s = glu.reshape(bd, sd, CONV_CH)
    ydw = _dwconv_sample(jnp.swapaxes(state_conv[0], 0, 1), jnp.swapaxes(glu_s, 0, 1), w_dw_pad, b_dw_row)
    ydw = jnp.swapaxes(ydw, 0, 1).reshape(ts, CONV_CH)
    y_sample = _trunk(xs, o, ydw, p_sample[0].reshape(ts, -1), wts, tm=ts, tb=64).reshape(bd, sd, D_MODEL)
    k_sample = k.reshape(1, bd, sd, N_HEADS, HEAD_DIM)
    v_sample = v.reshape(1, bd, sd, N_HEADS, HEAD_DIM)
    lf_sample = lf.T.reshape(1, bd, sd, N_HEADS)
    conv_sample = jnp.concatenate([state_conv[0][:, sd:], glu_s], axis=1)[None]

    return (y_prompt, y_sample, k_prompt, v_prompt, lf_prompt, conv_prompt,
            k_sample, v_sample, lf_sample, conv_sample)
```

```python
import functools

import jax
import jax.numpy as jnp
from jax import lax
from jax.experimental import pallas as pl
from jax.experimental.pallas import tpu as pltpu

F32 = jnp.float32
BF16 = jnp.bfloat16
I32 = jnp.int32

D_MODEL = 1024
N_HEADS = 8
HEAD_DIM = 64
ATT_WIDTH = N_HEADS * HEAD_DIM
CONV_CH = D_MODEL // 2
CONV_WIDTH = 31
PEER_HEADS = 8
N_KEYS = 128
HALF_KEY = 128
TOPK = 16
N_EXPERTS = N_KEYS * N_KEYS
RMS_EPS = 1e-6
LN_EPS = 1e-5

V7X_LANES = 128
V7X_SUBLANES = 8
V7X_VMEM_LIMIT_BYTES = 56 * 1024 * 1024

NEG_INF = float("-inf")
HIGHEST = lax.Precision.HIGHEST
NT_DIMS = (((1,), (1,)), ((), ()))


def _cparams(*sem):
    return pltpu.CompilerParams(dimension_semantics=sem, vmem_limit_bytes=V7X_VMEM_LIMIT_BYTES)


def _const_spec(shape):
    nd = len(shape)
    return pl.BlockSpec(shape, lambda *_: (0,) * nd, pipeline_mode=pl.Buffered(1))


def _rms(x, g):
    return x * lax.rsqrt(jnp.mean(x * x, axis=-1, keepdims=True) + RMS_EPS) * g


def _log_sigmoid(z):
    return jnp.minimum(z, 0.0) - jnp.log1p(jnp.exp(-jnp.abs(z)))


def _inproj_kernel(x_ref, g_ref, wqkv_ref, wf_ref, bf_ref, wglu_ref,
                   q_ref, k_ref, v_ref, kb_ref, vb_ref, lf_ref, cum_ref, glu_ref, carry_ref,
                   *, tm, seg):
    i = pl.program_id(0)
    u = _rms(x_ref[...], g_ref[...]).astype(BF16)
    qkv = jnp.dot(u, wqkv_ref[...], preferred_element_type=F32)
    q_ref[...] = (qkv[:, :ATT_WIDTH] * (HEAD_DIM ** -0.5)).astype(BF16)
    k = qkv[:, ATT_WIDTH:2 * ATT_WIDTH]
    v = qkv[:, 2 * ATT_WIDTH:]
    k_ref[...] = k
    v_ref[...] = v
    kb_ref[...] = k.astype(BF16)
    vb_ref[...] = v.astype(BF16)
    cab = jnp.dot(u, wglu_ref[...], preferred_element_type=F32)
    glu_ref[...] = cab[:, :CONV_CH] * jax.nn.sigmoid(cab[:, CONV_CH:])
    fl = lax.dot_general(wf_ref[...], u, NT_DIMS, preferred_element_type=F32)
    lf = _log_sigmoid(fl + bf_ref[...])
    lf_ref[...] = lf
    src = lax.broadcasted_iota(I32, (tm, tm), 0)
    dst = lax.broadcasted_iota(I32, (tm, tm), 1)
    keep = src <= dst
    if seg < tm:
        keep = keep & ((src // seg) == (dst // seg))
    tri = jnp.where(keep, 1.0, 0.0).astype(F32)
    cum = jnp.dot(lf, tri, preferred_element_type=F32, precision=HIGHEST)
    if seg > tm:
        @pl.when(i % (seg // tm) == 0)
        def _():
            carry_ref[...] = jnp.zeros_like(carry_ref)
        cum = cum + carry_ref[:, 0:1]
        carry_ref[...] = jnp.broadcast_to(cum[:, tm - 1:tm], carry_ref.shape)
    cum_ref[...] = cum


def _in_projection(x, g_mix, wqkv, wf_t, b_f, wglu, *, seg, tm):
    t = x.shape[0]
    row = lambda w: pl.BlockSpec((tm, w), lambda i: (i, 0))
    col = pl.BlockSpec((N_HEADS, tm), lambda i: (0, i))
    out_shape = (
        jax.ShapeDtypeStruct((t, ATT_WIDTH), BF16),
        jax.ShapeDtypeStruct((t, ATT_WIDTH), F32),
        jax.ShapeDtypeStruct((t, ATT_WIDTH), F32),
        jax.ShapeDtypeStruct((t, ATT_WIDTH), BF16),
        jax.ShapeDtypeStruct((t, ATT_WIDTH), BF16),
        jax.ShapeDtypeStruct((N_HEADS, t), F32),
        jax.ShapeDtypeStruct((N_HEADS, t), F32),
        jax.ShapeDtypeStruct((t, CONV_CH), F32),
    )
    return pl.pallas_call(
        functools.partial(_inproj_kernel, tm=tm, seg=seg),
        grid=(t // tm,),
        in_specs=[row(D_MODEL), _const_spec((1, D_MODEL)), _const_spec(wqkv.shape),
                  _const_spec(wf_t.shape), _const_spec((N_HEADS, 1)), _const_spec(wglu.shape)],
        out_specs=(row(ATT_WIDTH), row(ATT_WIDTH), row(ATT_WIDTH), row(ATT_WIDTH), row(ATT_WIDTH),
                   col, col, row(CONV_CH)),
        out_shape=out_shape,
        scratch_shapes=[pltpu.VMEM((N_HEADS, V7X_LANES), F32)],
        compiler_params=_cparams("arbitrary"),
        name="in_projection",
    )(x, g_mix, wqkv, wf_t, b_f, wglu)


def _fox_prompt_kernel(q_ref, k_ref, v_ref, cum_ref, o_ref, m_ref, l_ref, acc_ref, *, tq):
    qi = pl.program_id(2)
    q = q_ref[...]
    lane = lax.broadcasted_iota(I32, q.shape, 1)
    first_head = lane < HEAD_DIM
    zero = jnp.zeros_like(q)
    q_heads = (jnp.where(first_head, q, zero), jnp.where(first_head, zero, q))
    m_ref[...] = jnp.full(m_ref.shape, NEG_INF, F32)
    l_ref[...] = jnp.zeros(l_ref.shape, F32)
    acc_ref[...] = jnp.zeros(acc_ref.shape, F32)

    def tile(kj, diagonal):
        ks = pl.multiple_of(kj * tq, tq)
        kt = k_ref[pl.ds(ks, tq), :]
        vt = v_ref[pl.ds(ks, tq), :]
        for h in range(2):
            s = lax.dot_general(q_heads[h], kt, NT_DIMS, preferred_element_type=F32)
            s = s - cum_ref[h:h + 1, pl.ds(ks, tq)]
            if diagonal:
                r = lax.broadcasted_iota(I32, s.shape, 0)
                c = lax.broadcasted_iota(I32, s.shape, 1)
                s = jnp.where(c <= r, s, NEG_INF)
            m_old = m_ref[h]
            m_new = jnp.maximum(m_old, jnp.max(s, axis=1, keepdims=True))
            alpha = jnp.exp(m_old - m_new)
            p = jnp.exp(s - m_new)
            l_ref[h] = alpha * l_ref[h] + jnp.sum(p, axis=1, keepdims=True)
            acc_ref[h] = alpha * acc_ref[h] + jnp.dot(p.astype(BF16), vt, preferred_element_type=F32)
            m_ref[h] = m_new

    def body(kj, carry):
        tile(kj, False)
        return carry

    lax.fori_loop(0, qi, body, 0)
    tile(qi, True)
    o0 = acc_ref[0] / l_ref[0]
    o1 = acc_ref[1] / l_ref[1]
    o_ref[...] = jnp.where(first_head, o0, o1).astype(o_ref.dtype)


def _fox_prompt(q, kb, vb, cum_pairs, *, batch, seq, tq):
    nq = seq // tq
    pairs = N_HEADS // 2
    return pl.pallas_call(
        functools.partial(_fox_prompt_kernel, tq=tq),
        grid=(batch, pairs, nq),
        in_specs=[
            pl.BlockSpec((tq, V7X_LANES), lambda b, p, i: (b * nq + i, p)),
            pl.BlockSpec((seq, V7X_LANES), lambda b, p, i: (b, p)),
            pl.BlockSpec((seq, V7X_LANES), lambda b, p, i: (b, p)),
            pl.BlockSpec((None, 2, seq), lambda b, p, i: (p, 0, b)),
        ],
        out_specs=pl.BlockSpec((tq, V7X_LANES), lambda b, p, i: (b * nq + i, p)),
        out_shape=jax.ShapeDtypeStruct(q.shape, BF16),
        scratch_shapes=[pltpu.VMEM((2, tq, 1), F32), pltpu.VMEM((2, tq, 1), F32),
                        pltpu.VMEM((2, tq, V7X_LANES), F32)],
        compiler_params=_cparams("arbitrary", "arbitrary", "arbitrary"),
        name="fox_prompt",
    )(q, kb, vb, cum_pairs)


def _suffix_sums(lf):
    lane = lax.broadcasted_iota(I32, lf.shape, 1)
    row = lax.broadcasted_iota(I32, lf.shape, 0)
    n_lanes, n_rows = lf.shape[1], lf.shape[0]
    inc = lf
    for s in (8, 16, 32, 64):
        inc = inc + jnp.where(lane < n_lanes - s, pltpu.roll(inc, n_lanes - s, 1), 0.0)
    tot = jnp.where(lane < N_HEADS, inc, 0.0)
    for s in (8, 16, 32, 64):
        tot = tot + jnp.where(lane >= s, pltpu.roll(tot, s, 1), 0.0)
    below = tot
    for s in (1, 2, 4):
        below = below + jnp.where(row < n_rows - s, pltpu.roll(below, n_rows - s, 0), 0.0)
    page_total = jnp.broadcast_to(below[0:1, :], lf.shape)
    return (inc - lf) + (below - tot), page_total


def _fox_sample_kernel(pt_ref, q_ref, kn_ref, vn_ref, bn_ref, *rest, n_steps, pp, page):
    del pt_ref
    k_refs, v_refs, lf_refs = rest[:pp], rest[pp:2 * pp], rest[2 * pp:3 * pp]
    o_ref, qb_ref, m_ref, l_ref, acc_ref, suf_ref = rest[3 * pp:]
    j = pl.program_id(1)
    rows = q_ref.shape[0]
    lane = lax.broadcasted_iota(I32, (rows, V7X_LANES), 1)
    row = lax.broadcasted_iota(I32, (rows, V7X_LANES), 0)
    own_head = (lane & (N_HEADS - 1)) == (row & (N_HEADS - 1))
    tiles = page * N_HEADS // V7X_LANES

    def update(s, vt):
        m_old = m_ref[...]
        m_new = jnp.maximum(m_old, jnp.max(s, axis=1, keepdims=True))
        alpha = jnp.exp(m_old - m_new)
        p = jnp.exp(s - m_new)
        l_ref[...] = alpha * l_ref[...] + jnp.sum(p, axis=1, keepdims=True)
        acc_ref[...] = alpha * acc_ref[...] + jnp.dot(p.astype(BF16), vt, preferred_element_type=F32)
        m_ref[...] = m_new

    @pl.when(j == 0)
    def _():
        qb_ref[...] = q_ref[...].astype(BF16)
        m_ref[...] = jnp.full(m_ref.shape, NEG_INF, F32)
        l_ref[...] = jnp.zeros(l_ref.shape, F32)
        acc_ref[...] = jnp.zeros(acc_ref.shape, F32)
        suf_ref[...] = jnp.zeros(suf_ref.shape, F32)
        s = lax.dot_general(qb_ref[...], kn_ref[...], NT_DIMS, preferred_element_type=F32) + bn_ref[...]
        causal = (lane < rows) & ((lane // N_HEADS) <= (row // N_HEADS))
        update(jnp.where(own_head & causal, s, NEG_INF), vn_ref[...])

    @pl.when(j > 0)
    def _():
        ks, vs, biases = [], [], []
        suffix = suf_ref[...]
        for i in range(pp):
            excl, page_total = _suffix_sums(lf_refs[i][...])
            excl = excl + suffix
            suffix = suffix + page_total
            biases += [excl[c:c + 1, :] for c in range(tiles)]
            ks.append(k_refs[i][...].reshape(page * N_HEADS, HEAD_DIM).astype(BF16))
            vs.append(v_refs[i][...].reshape(page * N_HEADS, HEAD_DIM).astype(BF16))
        suf_ref[...] = suffix
        s = lax.dot_general(qb_ref[...], jnp.concatenate(ks, axis=0), NT_DIMS, preferred_element_type=F32)
        keep = jnp.concatenate([own_head] * (tiles * pp), axis=1)
        update(jnp.where(keep, s + jnp.concatenate(biases, axis=1), NEG_INF), jnp.concatenate(vs, axis=0))

    @pl.when(j == n_steps - 1)
    def _():
        o_ref[...] = acc_ref[...] / l_ref[...]


def _fox_sample(q, kn, vn, bn, cache_k, cache_v, cache_lf, page_table, *, pp):
    bd, rows, _ = q.shape
    n_pages = page_table.shape[1]
    page = cache_k.shape[1]
    n_steps = n_pages // pp + 1
    per_seq = lambda b, j, pt: (b, 0, 0)

    def pool_idx(i, nd):
        def index(b, j, pt):
            return (pt[b, n_pages - 1 - ((jnp.maximum(j, 1) - 1) * pp + i)],) + (0,) * nd
        return index

    kv_spec = lambda i: pl.BlockSpec((None, page, N_HEADS, HEAD_DIM), pool_idx(i, 3))
    lf_spec = lambda i: pl.BlockSpec((None,) + cache_lf.shape[1:], pool_idx(i, 2))
    grid_spec = pltpu.PrefetchScalarGridSpec(
        num_scalar_prefetch=1,
        grid=(bd, n_steps),
        in_specs=[pl.BlockSpec((None, rows, HEAD_DIM), per_seq),
                  pl.BlockSpec((None,) + kn.shape[1:], per_seq),
                  pl.BlockSpec((None,) + vn.shape[1:], per_seq),
                  pl.BlockSpec((None,) + bn.shape[1:], per_seq)]
                 + [kv_spec(i) for i in range(pp)] + [kv_spec(i) for i in range(pp)]
                 + [lf_spec(i) for i in range(pp)],
        out_specs=pl.BlockSpec((None, rows, HEAD_DIM), per_seq),
        scratch_shapes=[pltpu.VMEM((rows, HEAD_DIM), BF16), pltpu.VMEM((rows, 1), F32),
                        pltpu.VMEM((rows, 1), F32), pltpu.VMEM((rows, HEAD_DIM), F32),
                        pltpu.VMEM(cache_lf.shape[1:], F32)],
    )
    return pl.pallas_call(
        functools.partial(_fox_sample_kernel, n_steps=n_steps, pp=pp, page=page),
        grid_spec=grid_spec,
        out_shape=jax.ShapeDtypeStruct((bd, rows, HEAD_DIM), F32),
        compiler_params=_cparams("arbitrary", "arbitrary"),
        name="fox_sample",
    )(page_table, q, kn, vn, bn, *([cache_k] * pp), *([cache_v] * pp), *([cache_lf] * pp))


HALO = 32


def _dwconv_prompt_kernel(glu_ref, halo_ref, w_ref, b_ref, y_ref, cat_ref, *, tm, tiles_per_seq):
    i = pl.program_id(0)
    starts_sequence = (i % tiles_per_seq) == 0
    cat_ref[0:HALO, :] = jnp.where(starts_sequence, 0.0, halo_ref[...])
    cat_ref[HALO:HALO + tm, :] = glu_ref[...]
    off = HALO - (CONV_WIDTH - 1)
    acc = jnp.broadcast_to(b_ref[...], (tm, CONV_CH))
    for w in range(CONV_WIDTH):
        acc = acc + cat_ref[pl.ds(off + w, tm), :] * w_ref[w:w + 1, :]
    y_ref[...] = acc


def _dwconv_prompt(glu, w_dw, b_dw, *, seq, tm):
    t = glu.shape[0]
    per_tile = tm // HALO
    return pl.pallas_call(
        functools.partial(_dwconv_prompt_kernel, tm=tm, tiles_per_seq=seq // tm),
        grid=(t // tm,),
        in_specs=[pl.BlockSpec((tm, CONV_CH), lambda i: (i, 0)),
                  pl.BlockSpec((HALO, CONV_CH), lambda i: (jnp.maximum(i * per_tile - 1, 0), 0)),
                  _const_spec(w_dw.shape), _const_spec(b_dw.shape)],
        out_specs=pl.BlockSpec((tm, CONV_CH), lambda i: (i, 0)),
        out_shape=jax.ShapeDtypeStruct(glu.shape, F32),
        scratch_shapes=[pltpu.VMEM((HALO + tm, CONV_CH), F32)],
        compiler_params=_cparams("arbitrary"),
        name="dwconv_prompt",
    )(glu, glu, w_dw, b_dw)


def _dwconv_sample_kernel(hist_ref, new_ref, w_ref, b_ref, y_ref, *, n_new):
    hist = CONV_WIDTH - 1
    for t in range(n_new):
        acc = jnp.broadcast_to(b_ref[...], y_ref.shape[1:])
        for w in range(CONV_WIDTH):
            pos = t + w
            row = hist_ref[pos] if pos < hist else new_ref[pos - hist]
            acc = acc + row * w_ref[w:w + 1, :]
        y_ref[t] = acc


def _dwconv_sample(hist_tm, new_tm, w_dw, b_dw):
    n_new = new_tm.shape[0]
    return pl.pallas_call(
        functools.partial(_dwconv_sample_kernel, n_new=n_new),
        out_shape=jax.ShapeDtypeStruct(new_tm.shape, F32),
        name="dwconv_sample",
    )(hist_tm, new_tm, w_dw, b_dw)


def _mix_kernel(x_ref, o_ref, y_ref, gmix_ref, wgate_ref, watt_ref, lng_ref, lnb_ref, wconv_ref,
                wout_ref, gffn_ref, wpq_ref, h_ref, xb_ref, qp_ref):
    x = x_ref[...]
    y = y_ref[...]
    mu = jnp.mean(y, axis=-1, keepdims=True)
    var = jnp.mean(jnp.square(y - mu), axis=-1, keepdims=True)
    z = (y - mu) * lax.rsqrt(var + LN_EPS) * lng_ref[...] + lnb_ref[...]
    z = z * jax.nn.sigmoid(z)
    conv = jnp.dot(z.astype(BF16), wconv_ref[...], preferred_element_type=F32)
    att = jnp.dot(o_ref[...].astype(BF16), watt_ref[...], preferred_element_type=F32)
    u = _rms(x, gmix_ref[...]).astype(BF16)
    gates = jnp.dot(u, wgate_ref[...], preferred_element_type=F32)
    merged = jax.nn.sigmoid(gates[:, :D_MODEL]) * att + jax.nn.sigmoid(gates[:, D_MODEL:]) * conv
    h = x + jnp.dot(merged.astype(BF16), wout_ref[...], preferred_element_type=F32)
    h_ref[...] = h
    xb = _rms(h, gffn_ref[...])
    xb_ref[...] = xb
    qp_ref[...] = jnp.dot(xb.astype(BF16), wpq_ref[...], preferred_element_type=F32)


def _mix(x, o, ydw, g_mix, wgate, watt, ln_g, ln_b, wconv, wout, g_ffn, wpq, *, tm):
    t = x.shape[0]
    row = lambda w: pl.BlockSpec((tm, w), lambda i: (i, 0))
    nq = wpq.shape[1]
    return pl.pallas_call(
        _mix_kernel,
        grid=(t // tm,),
        in_specs=[row(D_MODEL), row(ATT_WIDTH), row(CONV_CH), _const_spec(g_mix.shape),
                  _const_spec(wgate.shape), _const_spec(watt.shape), _const_spec(ln_g.shape),
                  _const_spec(ln_b.shape), _const_spec(wconv.shape), _const_spec(wout.shape),
                  _const_spec(g_ffn.shape), _const_spec(wpq.shape)],
        out_specs=(row(D_MODEL), row(D_MODEL), row(nq)),
        out_shape=(jax.ShapeDtypeStruct((t, D_MODEL), F32), jax.ShapeDtypeStruct((t, D_MODEL), F32),
                   jax.ShapeDtypeStruct((t, nq), F32)),
        compiler_params=_cparams("arbitrary"),
        name="mix",
    )(x, o, ydw, g_mix, wgate, watt, ln_g, ln_b, wconv, wout, g_ffn, wpq)


CAND_ROWS = 56
ROUTE_GROUP = 2


def _candidate_tables(width):
    assert TOPK == 16
    k = TOPK
    pairs = ([(0, j) for j in range(16)] + [(1, j) for j in range(8)]
             + [(i, 0) if i >= 2 else None for i in range(16)] + [(i, 1) if i >= 2 else None for i in range(8)]
             + [(2, 2), (2, 3), (2, 4), (3, 2), (3, 3), (4, 2), None, None])
    assert len(pairs) == CAND_ROWS
    assert sorted(p for p in pairs if p) == sorted((i, j) for i in range(k) for j in range(k) if (i + 1) * (j + 1) <= k)
    pos = [[p[0] * k + p[1] if p else k * k + r] * width for r, p in enumerate(pairs)]
    pad = [[0.0 if p else NEG_INF] * width for p in pairs]
    return jnp.asarray(pos, I32), jnp.asarray(pad, F32)


def _pair_combine(a, b, op):
    t = a.shape[1]
    bc = lambda x, i, n: jnp.broadcast_to(x[i:i + 1, :], (n, t))
    r = lax.broadcasted_iota(I32, (V7X_SUBLANES, t), 0)
    ea = jnp.where(r < 3, bc(a, 2, 8), jnp.where(r < 5, bc(a, 3, 8), bc(a, 4, 8)))
    eb = jnp.where((r == 0) | (r == 3) | (r == 5), bc(b, 2, 8), jnp.where((r == 1) | (r == 4), bc(b, 3, 8), bc(b, 4, 8)))
    return jnp.concatenate([op(bc(a, 0, 16), b), op(bc(a, 1, 8), b[:8]), op(a, bc(b, 0, 16)),
                            op(a[:8], bc(b, 1, 8)), op(ea, eb)], axis=0)


def _topk_rows(x, k, order=None, payload=None):
    n, t = x.shape
    if order is None:
        order = lax.broadcasted_iota(I32, (n, t), 0)
    slot = lax.broadcasted_iota(I32, (k, t), 0)
    big = jnp.iinfo(jnp.int32).max

    def step(i, carry):
        x, vals, picks = carry
        m = jnp.max(x, axis=0, keepdims=True)
        first = jnp.min(jnp.where(x == m, order, big), axis=0, keepdims=True)
        hit = order == first
        pick = first if payload is None else jnp.max(jnp.where(hit, payload, -1), axis=0, keepdims=True)
        vals = jnp.where(slot == i, m, vals)
        picks = jnp.where(slot == i, pick, picks)
        return jnp.where(hit, NEG_INF, x), vals, picks

    _, vals, picks = lax.fori_loop(0, k, step, (x, jnp.zeros((k, t), F32), jnp.zeros((k, t), I32)))
    return vals, picks


def _peer_route_kernel(qp_ref, keys_ref, pos_ref, pad_ref, e_ref, g_ref, sv_ref, si_ref):
    tm = qp_ref.shape[0]

    def head(h, carry):
        s = []
        for p in range(2):
            c0 = pl.multiple_of((h * 2 + p) * HALF_KEY, HALF_KEY)
            q_hp = qp_ref[:, pl.ds(c0, HALF_KEY)].astype(BF16)
            s.append(lax.dot_general(keys_ref[p], q_hp, NT_DIMS, preferred_element_type=F32))
        sv_ref[h], si_ref[h] = _topk_rows(jnp.concatenate(s, axis=1), TOPK)
        return carry

    lax.fori_loop(0, PEER_HEADS, head, 0)
    for h0 in range(0, PEER_HEADS, ROUTE_GROUP):
        heads = range(h0, h0 + ROUTE_GROUP)
        side = lambda ref, p: jnp.concatenate([ref[h][:, p * tm:(p + 1) * tm] for h in heads], axis=1)
        comb = _pair_combine(side(sv_ref, 0), side(sv_ref, 1), lambda x, y: x + y) + pad_ref[...]
        cidx = _pair_combine(side(si_ref, 0), side(si_ref, 1), lambda x, y: x * N_KEYS + y)
        tv, eidx = _topk_rows(comb, TOPK, order=pos_ref[...], payload=cidx)
        ex = jnp.exp(tv - jnp.max(tv, axis=0, keepdims=True))
        gate = ex / jnp.sum(ex, axis=0, keepdims=True)
        for n, h in enumerate(heads):
            e_ref[h * TOPK:(h + 1) * TOPK, :] = eidx[:, n * tm:(n + 1) * tm] * HALF_ROWS
            g_ref[h * TOPK:(h + 1) * TOPK, :] = gate[:, n * tm:(n + 1) * tm]


def _peer_route(qp, keys):
    t = qp.shape[0]
    tm = V7X_LANES
    rows = PEER_HEADS * TOPK
    tables = _candidate_tables(ROUTE_GROUP * tm)
    out = pl.BlockSpec((rows, tm), lambda i: (0, i))
    return pl.pallas_call(
        _peer_route_kernel,
        grid=(t // tm,),
        in_specs=[pl.BlockSpec((tm, qp.shape[1]), lambda i: (i, 0)), _const_spec(keys.shape)]
                 + [_const_spec(a.shape) for a in tables],
        out_specs=(out, out),
        out_shape=(jax.ShapeDtypeStruct((rows, t), I32), jax.ShapeDtypeStruct((rows, t), F32)),
        scratch_shapes=[pltpu.VMEM((PEER_HEADS, TOPK, 2 * tm), F32), pltpu.VMEM((PEER_HEADS, TOPK, 2 * tm), I32)],
        compiler_params=_cparams("arbitrary"),
        name="peer_route",
    )(qp, keys, *tables)


PAIRS = PEER_HEADS * TOPK
HALF_ROWS = D_MODEL // 2 // V7X_LANES
HI_MASK = -65536


def _pack_expert_table(tab):
    bits = lax.bitcast_convert_type(tab.astype(BF16), jnp.uint16).astype(jnp.uint32)
    word = bits[:, :D_MODEL // 2] | (bits[:, D_MODEL // 2:] << 16)
    flat = lax.bitcast_convert_type(word, I32).reshape(tab.shape[0] * HALF_ROWS, V7X_LANES)
    return jnp.pad(flat, ((0, V7X_SUBLANES - HALF_ROWS), (0, 0)))


def _expert_halves(tab_ref, off):
    w = tab_ref[pl.ds(off, V7X_SUBLANES), :]
    return pltpu.bitcast(w << 16, F32), pltpu.bitcast(w & HI_MASK, F32)


def _split_halves(x):
    t = x.shape[0]
    return jnp.pad(x.reshape(t, 2, HALF_ROWS, V7X_LANES), ((0, 0), (0, 0), (0, V7X_SUBLANES - HALF_ROWS), (0, 0)))


def _pair_row_sums(ps):
    sub = lax.broadcasted_iota(I32, ps[0].shape, 0)
    stack = lambda a, b: ps[a] + pltpu.roll(ps[b], HALF_ROWS, 0)
    level = [stack(0, 4), stack(2, 6), stack(1, 5), stack(3, 7)]
    for h in (2, 1):
        first = (sub & (2 * h - 1)) < h
        level = [jnp.where(first, x, pltpu.roll(y, h, 0)) + jnp.where(first, pltpu.roll(x, V7X_SUBLANES - h, 0), y)
                 for x, y in zip(level[0::2], level[1::2])]
    return level[0]


def _peer_dot_kernel(off_ref, x_ref, g_ref, tab_ref, w_ref, r_ref, a_ref, *, tb):
    def token(t, carry):
        x_lo = x_ref[t, 0]
        x_hi = x_ref[t, 1]
        offs = off_ref.at[t]
        base = pl.multiple_of(t * PAIRS, PAIRS)
        for grp in range(PAIRS // V7X_SUBLANES):
            ps = []
            for j in range(V7X_SUBLANES):
                lo, hi = _expert_halves(tab_ref, offs[grp * V7X_SUBLANES + j])
                ps.append(lo * x_lo + hi * x_hi)
            r_ref[pl.ds(base + grp * V7X_SUBLANES, V7X_SUBLANES), :] = _pair_row_sums(ps)
        return carry

    lax.fori_loop(0, tb, token, 0)
    ones = jnp.ones((V7X_SUBLANES, V7X_LANES), BF16)
    r = r_ref[...]
    hi = r.astype(BF16)
    lo = (r - hi.astype(F32)).astype(BF16)
    sums = (lax.dot_general(ones, hi, NT_DIMS, preferred_element_type=F32)
            + lax.dot_general(ones, lo, NT_DIMS, preferred_element_type=F32))
    for t in range(tb):
        a_ref[t:t + 1, :] = sums[0:1, t * PAIRS:(t + 1) * PAIRS]
    a = a_ref[...]
    gelu = a * (lax.erf(a * (2.0 ** -0.5)) + 1.0) * 0.5
    w_ref[...] = g_ref[...] * gelu


def _peer_dot(off, x_halves, g, tab, *, tb):
    t = off.shape[0]
    return pl.pallas_call(
        functools.partial(_peer_dot_kernel, tb=tb),
        grid=(t // tb,),
        in_specs=[pl.BlockSpec((tb, PAIRS), lambda i: (i, 0), memory_space=pltpu.SMEM),
                  pl.BlockSpec((tb, 2, V7X_SUBLANES, V7X_LANES), lambda i: (i, 0, 0, 0)),
                  pl.BlockSpec((tb, PAIRS), lambda i: (i, 0)),
                  _const_spec(tab.shape)],
        out_specs=pl.BlockSpec((tb, PAIRS), lambda i: (i, 0)),
        out_shape=jax.ShapeDtypeStruct((t, PAIRS), F32),
        scratch_shapes=[pltpu.VMEM((tb * PAIRS, V7X_LANES), F32), pltpu.VMEM((tb, PAIRS), F32)],
        compiler_params=_cparams("arbitrary"),
        name="peer_dot",
    )(off, x_halves, g, tab)


def _peer_sum_kernel(off_ref, w_ref, tab_ref, y_ref, *, tb):
    n_acc = 2

    def token(t, carry):
        zero = jnp.zeros((V7X_SUBLANES, V7X_LANES), F32)
        acc_lo, acc_hi = [zero] * n_acc, [zero] * n_acc
        offs = off_ref.at[t]
        ws = w_ref.at[t]
        for k in range(PAIRS):
            lo, hi = _expert_halves(tab_ref, offs[k])
            acc_lo[k % n_acc] = acc_lo[k % n_acc] + ws[k] * lo
            acc_hi[k % n_acc] = acc_hi[k % n_acc] + ws[k] * hi
        y_ref[t, 0] = (acc_lo[0] + acc_lo[1])[:HALF_ROWS]
        y_ref[t, 1] = (acc_hi[0] + acc_hi[1])[:HALF_ROWS]
        return carry

    lax.fori_loop(0, tb, token, 0)


def _peer_sum(off, w, tab, *, tb):
    t = off.shape[0]
    smem = pl.BlockSpec((tb, PAIRS), lambda i: (i, 0), memory_space=pltpu.SMEM)
    return pl.pallas_call(
        functools.partial(_peer_sum_kernel, tb=tb),
        grid=(t // tb,),
        in_specs=[smem, smem, _const_spec(tab.shape)],
        out_specs=pl.BlockSpec((tb, 2, HALF_ROWS, V7X_LANES), lambda i: (i, 0, 0, 0)),
        out_shape=jax.ShapeDtypeStruct((t, 2, HALF_ROWS, V7X_LANES), F32),
        compiler_params=_cparams("arbitrary"),
        name="peer_sum",
    )(off, w, tab)


def _ple_kernel(h_ref, y_ref, p_ref, gple_ref, wple_ref, wgate_ref, gfin_ref, o_ref):
    h = h_ref[...] + y_ref[...]
    gate = jax.nn.sigmoid(jnp.dot(_rms(h, gple_ref[...]).astype(BF16), wgate_ref[...], preferred_element_type=F32))
    h = h + jnp.dot(p_ref[...].astype(BF16), wple_ref[...], preferred_element_type=F32) * gate
    o_ref[...] = _rms(h, gfin_ref[...])


def _ple(h, y, p, g_ple, wple, wgate, g_final, *, tm):
    t = h.shape[0]
    row = lambda w: pl.BlockSpec((tm, w), lambda i: (i, 0))
    return pl.pallas_call(
        _ple_kernel,
        grid=(t // tm,),
        in_specs=[row(D_MODEL), row(D_MODEL), row(p.shape[1]), _const_spec(g_ple.shape),
                  _const_spec(wple.shape), _const_spec(wgate.shape), _const_spec(g_final.shape)],
        out_specs=row(D_MODEL),
        out_shape=jax.ShapeDtypeStruct((t, D_MODEL), F32),
        compiler_params=_cparams("arbitrary"),
        name="ple_out",
    )(h, y, p, g_ple, wple, wgate, g_final)


def _trunk(x, o, ydw, p, wts, *, tm, tb):
    h, xb, qp = _mix(x, o, ydw, wts["g_mix"], wts["wgate"], wts["watt"], wts["ln_g"], wts["ln_b"],
                     wts["wconv"], wts["wout"], wts["g_ffn"], wts["wpq"], tm=tm)
    e_t, g_t = _peer_route(qp, wts["keys"])
    off = e_t.T
    w = _peer_dot(off, _split_halves(xb), g_t.T, wts["tab_u"], tb=tb)
    y = _peer_sum(off, w, wts["tab_v"], tb=tb).reshape(x.shape[0], D_MODEL)
    return _ple(h, y, p, wts["g_ple"], wts["wple"], wts["wplegate"], wts["g_final"], tm=tm)


def kernel(x_prompt, x_sample, cache_k, cache_v, cache_lf, state_conv, page_table, p_prompt, p_sample,
           g_mix, w_in, b_f, w_dw, b_dw, ln_g, ln_b, w_conv_proj, w_att_proj, w_out, g_ffn, w_pq,
           sub_keys, exp_u, exp_v, g_ple, w_ple, w_ple_gate, g_final):
    assert g_mix.shape[0] == 1, "single-layer trunk"
    b, s, _ = x_prompt.shape
    bd, sd, _ = x_sample.shape
    tp, ts = b * s, bd * sd
    page = cache_k.shape[2]

    w = w_in[0]
    c_f, c_glu, c_gate = 3 * ATT_WIDTH, 3 * ATT_WIDTH + N_HEADS, 3 * ATT_WIDTH + N_HEADS + 2 * CONV_CH
    wqkv = w[:, :c_f].astype(BF16)
    wf_t = w[:, c_f:c_glu].T.astype(BF16)
    wglu = w[:, c_glu:c_gate].astype(BF16)
    row = lambda a: a.reshape(1, -1)
    w_dw_pad = jnp.pad(w_dw[0], ((0, 1), (0, 0)))
    wts = dict(
        g_mix=row(g_mix[0]), wgate=w[:, c_gate:].astype(BF16), watt=w_att_proj[0].astype(BF16),
        ln_g=row(ln_g[0]), ln_b=row(ln_b[0]), wconv=w_conv_proj[0].astype(BF16), wout=w_out[0].astype(BF16),
        g_ffn=row(g_ffn[0]), wpq=w_pq[0].astype(BF16), keys=sub_keys[0].astype(BF16),
        tab_u=_pack_expert_table(exp_u[0]), tab_v=_pack_expert_table(exp_v[0]),
        g_ple=row(g_ple[0]), wple=w_ple[0].astype(BF16), wplegate=w_ple_gate[0].astype(BF16),
        g_final=row(g_final))
    b_f_col = b_f[0].reshape(N_HEADS, 1)
    b_dw_row = row(b_dw[0])

    xp = x_prompt.reshape(tp, D_MODEL)
    q, k, v, kb, vb, lf, cum, glu = _in_projection(xp, wts["g_mix"], wqkv, wf_t, b_f_col, wglu, seg=s, tm=512)
    o = _fox_prompt(q, kb, vb, cum.reshape(N_HEADS // 2, 2, tp), batch=b, seq=s, tq=512)
    ydw = _dwconv_prompt(glu, w_dw_pad, b_dw_row, seq=s, tm=512)
    y_prompt = _trunk(xp, o, ydw, p_prompt[0].reshape(tp, -1), wts, tm=256, tb=64).reshape(b, s, D_MODEL)
    k_prompt = k.reshape(1, b, s, N_HEADS, HEAD_DIM)
    v_prompt = v.reshape(1, b, s, N_HEADS, HEAD_DIM)
    lf_prompt = lf.T.reshape(1, b, s, N_HEADS)
    conv_prompt = glu.reshape(b, s, CONV_CH)[None, :, s - (CONV_WIDTH - 1):, :]

    xs = x_sample.reshape(ts, D_MODEL)
    q, k, v, kb, vb, lf, cum, glu = _in_projection(xs, wts["g_mix"], wqkv, wf_t, b_f_col, wglu, seg=sd, tm=ts)
    rows = sd * N_HEADS
    key_rows = lambda a: jnp.pad(a.reshape(bd, rows, HEAD_DIM), ((0, 0), (0, V7X_LANES - rows), (0, 0)))
    bn = jnp.pad(-cum.reshape(N_HEADS, bd, sd).transpose(1, 2, 0).reshape(bd, 1, rows),
                 ((0, 0), (0, 0), (0, V7X_LANES - rows)))
    n_pool = cache_k.shape[1]
    o = _fox_sample(q.astype(F32).reshape(bd, rows, HEAD_DIM), key_rows(kb), key_rows(vb), bn,
                    cache_k[0], cache_v[0], cache_lf[0].reshape(n_pool, page * N_HEADS // V7X_LANES, V7X_LANES),
                    page_table, pp=4).reshape(ts, ATT_WIDTH)
    glu_s = glu.reshape(bd, sd, CONV_CH)
    ydw = _dwconv_sample(jnp.swapaxes(state_conv[0], 0, 1), jnp.swapaxes(glu_s, 0, 1), w_dw_pad, b_dw_row)
    ydw = jnp.swapaxes(ydw, 0, 1).reshape(ts, CONV_CH)
    y_sample = _trunk(xs, o, ydw, p_sample[0].reshape(ts, -1), wts, tm=ts, tb=64).reshape(bd, sd, D_MODEL)
    k_sample = k.reshape(1, bd, sd, N_HEADS, HEAD_DIM)
    v_sample = v.reshape(1, bd, sd, N_HEADS, HEAD_DIM)
    lf_sample = lf.T.reshape(1, bd, sd, N_HEADS)
    conv_sample = jnp.concatenate([state_conv[0][:, sd:], glu_s], axis=1)[None]

    return (y_prompt, y_sample, k_prompt, v_prompt, lf_prompt, conv_prompt,
            k_sample, v_sample, lf_sample, conv_sample)
```

```python
import functools

import jax
import jax.numpy as jnp
from jax import lax
from jax.experimental import pallas as pl
from jax.experimental.pallas import tpu as pltpu

F32 = jnp.float32
BF16 = jnp.bfloat16
I32 = jnp.int32

D_MODEL = 1024
N_HEADS = 8
HEAD_DIM = 64
ATT_WIDTH = N_HEADS * HEAD_DIM
CONV_CH = D_MODEL // 2
CONV_WIDTH = 31
PEER_HEADS = 8
N_KEYS = 128
HALF_KEY = 128
TOPK = 16
N_EXPERTS = N_KEYS * N_KEYS
RMS_EPS = 1e-6
LN_EPS = 1e-5

V7X_LANES = 128
V7X_SUBLANES = 8
V7X_VMEM_LIMIT_BYTES = 56 * 1024 * 1024

NEG_INF = float("-inf")
HIGHEST = lax.Precision.HIGHEST
NT_DIMS = (((1,), (1,)), ((), ()))


def _cparams(*sem):
    return pltpu.CompilerParams(dimension_semantics=sem, vmem_limit_bytes=V7X_VMEM_LIMIT_BYTES)


def _const_spec(shape):
    nd = len(shape)
    return pl.BlockSpec(shape, lambda *_: (0,) * nd, pipeline_mode=pl.Buffered(1))


def _rms(x, g):
    return x * lax.rsqrt(jnp.mean(x * x, axis=-1, keepdims=True) + RMS_EPS) * g


def _log_sigmoid(z):
    return jnp.minimum(z, 0.0) - jnp.log1p(jnp.exp(-jnp.abs(z)))


def _inproj_kernel(x_ref, g_ref, wqkv_ref, wf_ref, bf_ref, wglu_ref,
                   q_ref, k_ref, v_ref, kb_ref, vb_ref, lf_ref, cum_ref, glu_ref, carry_ref,
                   *, tm, seg, q_scale):
    i = pl.program_id(0)
    u = _rms(x_ref[...], g_ref[...]).astype(BF16)
    qkv = jnp.dot(u, wqkv_ref[...], preferred_element_type=F32)
    q_ref[...] = (qkv[:, :ATT_WIDTH] * q_scale).astype(BF16)
    k = qkv[:, ATT_WIDTH:2 * ATT_WIDTH]
    v = qkv[:, 2 * ATT_WIDTH:]
    k_ref[...] = k
    v_ref[...] = v
    kb_ref[...] = k.astype(BF16)
    vb_ref[...] = v.astype(BF16)
    cab = jnp.dot(u, wglu_ref[...], preferred_element_type=F32)
    glu_ref[...] = cab[:, :CONV_CH] * jax.nn.sigmoid(cab[:, CONV_CH:])
    fl = lax.dot_general(wf_ref[...], u, NT_DIMS, preferred_element_type=F32)
    lf = _log_sigmoid(fl + bf_ref[...])
    lf_ref[...] = lf
    src = lax.broadcasted_iota(I32, (tm, tm), 0)
    dst = lax.broadcasted_iota(I32, (tm, tm), 1)
    keep = src <= dst
    if seg < tm:
        keep = keep & ((src // seg) == (dst // seg))
    tri = jnp.where(keep, 1.0, 0.0).astype(F32)
    cum = jnp.dot(lf, tri, preferred_element_type=F32, precision=HIGHEST)
    if seg > tm:
        @pl.when(i % (seg // tm) == 0)
        def _():
            carry_ref[...] = jnp.zeros_like(carry_ref)
        cum = cum + carry_ref[:, 0:1]
        carry_ref[...] = jnp.broadcast_to(cum[:, tm - 1:tm], carry_ref.shape)
    cum_ref[...] = cum


def _in_projection(x, g_mix, wqkv, wf_t, b_f, wglu, *, seg, tm, q_scale):
    t = x.shape[0]
    row = lambda w: pl.BlockSpec((tm, w), lambda i: (i, 0))
    col = pl.BlockSpec((N_HEADS, tm), lambda i: (0, i))
    out_shape = (
        jax.ShapeDtypeStruct((t, ATT_WIDTH), BF16),
        jax.ShapeDtypeStruct((t, ATT_WIDTH), F32),
        jax.ShapeDtypeStruct((t, ATT_WIDTH), F32),
        jax.ShapeDtypeStruct((t, ATT_WIDTH), BF16),
        jax.ShapeDtypeStruct((t, ATT_WIDTH), BF16),
        jax.ShapeDtypeStruct((N_HEADS, t), F32),
        jax.ShapeDtypeStruct((N_HEADS, t), F32),
        jax.ShapeDtypeStruct((t, CONV_CH), F32),
    )
    return pl.pallas_call(
        functools.partial(_inproj_kernel, tm=tm, seg=seg, q_scale=q_scale),
        grid=(t // tm,),
        in_specs=[row(D_MODEL), _const_spec((1, D_MODEL)), _const_spec(wqkv.shape),
                  _const_spec(wf_t.shape), _const_spec((N_HEADS, 1)), _const_spec(wglu.shape)],
        out_specs=(row(ATT_WIDTH), row(ATT_WIDTH), row(ATT_WIDTH), row(ATT_WIDTH), row(ATT_WIDTH),
                   col, col, row(CONV_CH)),
        out_shape=out_shape,
        scratch_shapes=[pltpu.VMEM((N_HEADS, V7X_LANES), F32)],
        compiler_params=_cparams("arbitrary"),
        name="in_projection",
    )(x, g_mix, wqkv, wf_t, b_f, wglu)


LOG2E = 1.4426950408889634
BIAS_PIECES = 3


def _fox_prompt_kernel(q_ref, k_ref, v_ref, cum_ref, o_ref, kaug_ref, vt_ref, m_ref, l_ref, acc_ref, *, tq):
    qi = pl.program_id(2)
    seq = k_ref.shape[0]
    lane_k = lax.broadcasted_iota(I32, (seq, V7X_LANES), 1)

    @pl.when(qi == 0)
    def _():
        vt_ref[...] = v_ref[...].astype(F32).T.astype(BF16)
        k = k_ref[...]
        for h in range(2):
            c = cum_ref[h:h + 1, :] * LOG2E
            pieces = []
            for _ in range(BIAS_PIECES):
                piece = c.astype(BF16).astype(F32)
                pieces.append(piece)
                c = c - piece
            rows = jnp.concatenate(pieces + [jnp.zeros((V7X_SUBLANES - BIAS_PIECES, seq), F32)], axis=0)
            spare = (1 - h) * HEAD_DIM
            place = jnp.where((lax.broadcasted_iota(I32, (V7X_SUBLANES, V7X_LANES), 1) - spare)
                              == lax.broadcasted_iota(I32, (V7X_SUBLANES, V7X_LANES), 0), 1.0, 0.0)
            place = jnp.where(lax.broadcasted_iota(I32, place.shape, 0) < BIAS_PIECES, place, 0.0)
            extra = lax.dot_general(rows, place, (((0,), (0,)), ((), ())), preferred_element_type=F32)
            own = (lane_k >= h * HEAD_DIM) & (lane_k < (h + 1) * HEAD_DIM)
            kaug_ref[h] = jnp.where(own, k, extra.astype(BF16))

    q = q_ref[...]
    lane_q = lax.broadcasted_iota(I32, q.shape, 1)
    q_aug = []
    for h in range(2):
        own = (lane_q >= h * HEAD_DIM) & (lane_q < (h + 1) * HEAD_DIM)
        spare = (1 - h) * HEAD_DIM
        minus_one = (lane_q >= spare) & (lane_q < spare + BIAS_PIECES)
        q_aug.append(jnp.where(own, q, jnp.where(minus_one, -1.0, 0.0).astype(q.dtype)))
    m_ref[...] = jnp.full(m_ref.shape, NEG_INF, F32)
    l_ref[...] = jnp.zeros(l_ref.shape, F32)
    acc_ref[...] = jnp.zeros(acc_ref.shape, F32)

    def tile(kj, diagonal):
        ks = pl.multiple_of(kj * tq, tq)
        vt = vt_ref[:, pl.ds(ks, tq)]
        for h in range(2):
            st = lax.dot_general(kaug_ref[h, pl.ds(ks, tq), :], q_aug[h], NT_DIMS,
                                 preferred_element_type=F32)
            if diagonal:
                key = lax.broadcasted_iota(I32, st.shape, 0)
                qry = lax.broadcasted_iota(I32, st.shape, 1)
                st = jnp.where(key <= qry, st, NEG_INF)
            m_old = m_ref[h]
            m_new = jnp.maximum(m_old, jnp.max(st, axis=0, keepdims=True))
            alpha = jnp.exp2(m_old - m_new)
            p = jnp.exp2(st - m_new)
            l_ref[h] = alpha * l_ref[h] + jnp.sum(p, axis=0, keepdims=True)
            acc_ref[h] = alpha * acc_ref[h] + jnp.dot(vt, p.astype(BF16), preferred_element_type=F32)
            m_ref[h] = m_new

    def body(kj, carry):
        tile(kj, False)
        return carry

    lax.fori_loop(0, qi, body, 0)
    tile(qi, True)
    o0 = acc_ref[0] / l_ref[0]
    o1 = acc_ref[1] / l_ref[1]
    first_head = lax.broadcasted_iota(I32, o0.shape, 0) < HEAD_DIM
    o_ref[...] = jnp.where(first_head, o0, o1).T.astype(o_ref.dtype)


def _fox_prompt(q, kb, vb, cum_pairs, *, batch, seq, tq):
    nq = seq // tq
    pairs = N_HEADS // 2
    return pl.pallas_call(
        functools.partial(_fox_prompt_kernel, tq=tq),
        grid=(batch, pairs, nq),
        in_specs=[
            pl.BlockSpec((tq, V7X_LANES), lambda b, p, i: (b * nq + i, p)),
            pl.BlockSpec((seq, V7X_LANES), lambda b, p, i: (b, p)),
            pl.BlockSpec((seq, V7X_LANES), lambda b, p, i: (b, p)),
            pl.BlockSpec((None, 2, seq), lambda b, p, i: (p, 0, b)),
        ],
        out_specs=pl.BlockSpec((tq, V7X_LANES), lambda b, p, i: (b * nq + i, p)),
        out_shape=jax.ShapeDtypeStruct(q.shape, BF16),
        scratch_shapes=[pltpu.VMEM((2, seq, V7X_LANES), BF16), pltpu.VMEM((V7X_LANES, seq), BF16),
                        pltpu.VMEM((2, 1, tq), F32), pltpu.VMEM((2, 1, tq), F32),
                        pltpu.VMEM((2, V7X_LANES, tq), F32)],
        compiler_params=_cparams("arbitrary", "arbitrary", "arbitrary"),
        name="fox_prompt",
    )(q, kb, vb, cum_pairs)


def _fox_sample_kernel(pt_ref, q_ref, kn_ref, vn_ref, cn_ref, *rest, n_steps, pp, n_new):
    del pt_ref
    k_refs, v_refs, lf_refs = rest[:pp], rest[pp:2 * pp], rest[2 * pp:3 * pp]
    o_ref, qbd_ref, m_ref, l_ref, acc_ref, suf_ref = rest[3 * pp:]
    j = pl.program_id(1)
    rows = n_new * N_HEADS
    page = suf_ref.shape[1]
    own_head = (lax.broadcasted_iota(I32, (N_HEADS, ATT_WIDTH), 1) // HEAD_DIM
                == lax.broadcasted_iota(I32, (N_HEADS, ATT_WIDTH), 0))

    def update(s, vt):
        m_old = m_ref[...]
        m_new = jnp.maximum(m_old, jnp.max(s, axis=1, keepdims=True))
        alpha = jnp.exp(m_old - m_new)
        p = jnp.exp(s - m_new)
        l_ref[...] = alpha * l_ref[...] + jnp.sum(p, axis=1, keepdims=True)
        acc_ref[...] = alpha * acc_ref[...] + lax.dot_general(p.astype(BF16), vt, NT_DIMS,
                                                              preferred_element_type=F32)
        m_ref[...] = m_new

    @pl.when(j == 0)
    def _():
        q = q_ref[...]
        qbd = jnp.concatenate(
            [jnp.where(own_head, jnp.broadcast_to(q[t:t + 1, :], own_head.shape), 0.0) for t in range(n_new)],
            axis=0)
        qbd_ref[...] = qbd.astype(BF16)
        m_ref[...] = jnp.full(m_ref.shape, NEG_INF, F32)
        l_ref[...] = jnp.zeros(l_ref.shape, F32)
        acc_ref[...] = jnp.zeros(acc_ref.shape, F32)
        suf_ref[...] = jnp.zeros(suf_ref.shape, F32)
        cn = jnp.concatenate([cn_ref[...]] * n_new, axis=0)
        tok = lax.broadcasted_iota(I32, (rows, page), 0) // N_HEADS
        col = lax.broadcasted_iota(I32, (rows, page), 1)
        s = jnp.dot(qbd_ref[...], kn_ref[...], preferred_element_type=F32)
        update(jnp.where(col <= tok, s - cn, NEG_INF), vn_ref[...])

    @pl.when(j > 0)
    def _():
        lane = lax.broadcasted_iota(I32, suf_ref.shape, 1)
        kts, vts, biases = [], [], []
        suffix = suf_ref[...]
        for i in range(pp):
            lf = lf_refs[i][...]
            inc = lf
            shift = 1
            while shift < page:
                inc = inc + jnp.where(lane < page - shift, pltpu.roll(inc, page - shift, 1), 0.0)
                shift *= 2
            biases.append(jnp.concatenate([(inc - lf) + suffix] * n_new, axis=0))
            suffix = suffix + jnp.broadcast_to(inc[:, 0:1], inc.shape)
            kts.append(k_refs[i][...].reshape(ATT_WIDTH, page).astype(BF16))
            vts.append(v_refs[i][...].reshape(ATT_WIDTH, page).astype(BF16))
        suf_ref[...] = suffix
        s = jnp.dot(qbd_ref[...], jnp.concatenate(kts, axis=1), preferred_element_type=F32)
        update(s + jnp.concatenate(biases, axis=1), jnp.concatenate(vts, axis=1))

    @pl.when(j == n_steps - 1)
    def _():
        o = acc_ref[...] / l_ref[...]
        out = [jnp.sum(jnp.where(own_head, o[t * N_HEADS:(t + 1) * N_HEADS, :], 0.0), axis=0, keepdims=True)
               for t in range(n_new)]
        o_ref[...] = jnp.concatenate(out, axis=0)


def _fox_sample(q, kn_t, vn_t, cn, cache_kt, cache_vt, cache_lft, page_table, *, pp):
    bd, n_new, _ = q.shape
    n_pages = page_table.shape[1]
    page = cache_kt.shape[3]
    rows = n_new * N_HEADS
    n_steps = n_pages // pp + 1
    per_seq = lambda b, j, pt: (b, 0, 0)

    def pool_idx(i, nd):
        def index(b, j, pt):
            return (pt[b, n_pages - 1 - ((jnp.maximum(j, 1) - 1) * pp + i)],) + (0,) * nd
        return index

    kv_spec = lambda i: pl.BlockSpec((None, N_HEADS, HEAD_DIM, page), pool_idx(i, 3))
    lf_spec = lambda i: pl.BlockSpec((None, N_HEADS, page), pool_idx(i, 2))
    grid_spec = pltpu.PrefetchScalarGridSpec(
        num_scalar_prefetch=1,
        grid=(bd, n_steps),
        in_specs=[pl.BlockSpec((None, n_new, ATT_WIDTH), per_seq),
                  pl.BlockSpec((None, ATT_WIDTH, page), per_seq),
                  pl.BlockSpec((None, ATT_WIDTH, page), per_seq),
                  pl.BlockSpec((None, N_HEADS, page), per_seq)]
                 + [kv_spec(i) for i in range(pp)] + [kv_spec(i) for i in range(pp)]
                 + [lf_spec(i) for i in range(pp)],
        out_specs=pl.BlockSpec((None, n_new, ATT_WIDTH), per_seq),
        scratch_shapes=[pltpu.VMEM((rows, ATT_WIDTH), BF16), pltpu.VMEM((rows, 1), F32),
                        pltpu.VMEM((rows, 1), F32), pltpu.VMEM((rows, ATT_WIDTH), F32),
                        pltpu.VMEM((N_HEADS, page), F32)],
    )
    return pl.pallas_call(
        functools.partial(_fox_sample_kernel, n_steps=n_steps, pp=pp, n_new=n_new),
        grid_spec=grid_spec,
        out_shape=jax.ShapeDtypeStruct((bd, n_new, ATT_WIDTH), F32),
        compiler_params=_cparams("arbitrary", "arbitrary"),
        name="fox_sample",
    )(page_table, q, kn_t, vn_t, cn, *([cache_kt] * pp), *([cache_vt] * pp), *([cache_lft] * pp))


HALO = 32


def _dwconv_prompt_kernel(glu_ref, halo_ref, w_ref, b_ref, y_ref, cat_ref, *, tm, tiles_per_seq):
    i = pl.program_id(0)
    starts_sequence = (i % tiles_per_seq) == 0
    cat_ref[0:HALO, :] = jnp.where(starts_sequence, 0.0, halo_ref[...])
    cat_ref[HALO:HALO + tm, :] = glu_ref[...]
    off = HALO - (CONV_WIDTH - 1)
    acc = jnp.broadcast_to(b_ref[...], (tm, CONV_CH))
    for w in range(CONV_WIDTH):
        acc = acc + cat_ref[pl.ds(off + w, tm), :] * w_ref[w:w + 1, :]
    y_ref[...] = acc


def _dwconv_prompt(glu, w_dw, b_dw, *, seq, tm):
    t = glu.shape[0]
    per_tile = tm // HALO
    return pl.pallas_call(
        functools.partial(_dwconv_prompt_kernel, tm=tm, tiles_per_seq=seq // tm),
        grid=(t // tm,),
        in_specs=[pl.BlockSpec((tm, CONV_CH), lambda i: (i, 0)),
                  pl.BlockSpec((HALO, CONV_CH), lambda i: (jnp.maximum(i * per_tile - 1, 0), 0)),
                  _const_spec(w_dw.shape), _const_spec(b_dw.shape)],
        out_specs=pl.BlockSpec((tm, CONV_CH), lambda i: (i, 0)),
        out_shape=jax.ShapeDtypeStruct(glu.shape, F32),
        scratch_shapes=[pltpu.VMEM((HALO + tm, CONV_CH), F32)],
        compiler_params=_cparams("arbitrary"),
        name="dwconv_prompt",
    )(glu, glu, w_dw, b_dw)


def _dwconv_sample_kernel(hist_ref, new_ref, w_ref, b_ref, y_ref, *, n_new):
    hist = CONV_WIDTH - 1
    for t in range(n_new):
        acc = jnp.broadcast_to(b_ref[...], y_ref.shape[1:])
        for w in range(CONV_WIDTH):
            pos = t + w
            row = hist_ref[pos] if pos < hist else new_ref[pos - hist]
            acc = acc + row * w_ref[w:w + 1, :]
        y_ref[t] = acc


def _dwconv_sample(hist_tm, new_tm, w_dw, b_dw):
    n_new = new_tm.shape[0]
    return pl.pallas_call(
        functools.partial(_dwconv_sample_kernel, n_new=n_new),
        out_shape=jax.ShapeDtypeStruct(new_tm.shape, F32),
        name="dwconv_sample",
    )(hist_tm, new_tm, w_dw, b_dw)


def _mix_kernel(x_ref, o_ref, y_ref, gmix_ref, wgate_ref, watt_ref, lng_ref, lnb_ref, wconv_ref,
                wout_ref, gffn_ref, wpq_ref, h_ref, xb_ref, qp_ref):
    x = x_ref[...]
    y = y_ref[...]
    mu = jnp.mean(y, axis=-1, keepdims=True)
    var = jnp.mean(jnp.square(y - mu), axis=-1, keepdims=True)
    z = (y - mu) * lax.rsqrt(var + LN_EPS) * lng_ref[...] + lnb_ref[...]
    z = z * jax.nn.sigmoid(z)
    conv = jnp.dot(z.astype(BF16), wconv_ref[...], preferred_element_type=F32)
    att = jnp.dot(o_ref[...].astype(BF16), watt_ref[...], preferred_element_type=F32)
    u = _rms(x, gmix_ref[...]).astype(BF16)
    gates = jnp.dot(u, wgate_ref[...], preferred_element_type=F32)
    merged = jax.nn.sigmoid(gates[:, :D_MODEL]) * att + jax.nn.sigmoid(gates[:, D_MODEL:]) * conv
    h = x + jnp.dot(merged.astype(BF16), wout_ref[...], preferred_element_type=F32)
    h_ref[...] = h
    xb = _rms(h, gffn_ref[...])
    xb_ref[...] = xb
    qp_ref[...] = jnp.dot(xb.astype(BF16), wpq_ref[...], preferred_element_type=F32)


def _mix(x, o, ydw, g_mix, wgate, watt, ln_g, ln_b, wconv, wout, g_ffn, wpq, *, tm):
    t = x.shape[0]
    row = lambda w: pl.BlockSpec((tm, w), lambda i: (i, 0))
    nq = wpq.shape[1]
    return pl.pallas_call(
        _mix_kernel,
        grid=(t // tm,),
        in_specs=[row(D_MODEL), row(ATT_WIDTH), row(CONV_CH), _const_spec(g_mix.shape),
                  _const_spec(wgate.shape), _const_spec(watt.shape), _const_spec(ln_g.shape),
                  _const_spec(ln_b.shape), _const_spec(wconv.shape), _const_spec(wout.shape),
                  _const_spec(g_ffn.shape), _const_spec(wpq.shape)],
        out_specs=(row(D_MODEL), row(D_MODEL), row(nq)),
        out_shape=(jax.ShapeDtypeStruct((t, D_MODEL), F32), jax.ShapeDtypeStruct((t, D_MODEL), F32),
                   jax.ShapeDtypeStruct((t, nq), F32)),
        compiler_params=_cparams("arbitrary"),
        name="mix",
    )(x, o, ydw, g_mix, wgate, watt, ln_g, ln_b, wconv, wout, g_ffn, wpq)


CAND_ROWS = 56
ROUTE_GROUP = 2


def _candidate_tables(width):
    assert TOPK == 16
    k = TOPK
    pairs = ([(0, j) for j in range(16)] + [(1, j) for j in range(8)]
             + [(i, 0) if i >= 2 else None for i in range(16)] + [(i, 1) if i >= 2 else None for i in range(8)]
             + [(2, 2), (2, 3), (2, 4), (3, 2), (3, 3), (4, 2), None, None])
    assert len(pairs) == CAND_ROWS
    assert sorted(p for p in pairs if p) == sorted((i, j) for i in range(k) for j in range(k) if (i + 1) * (j + 1) <= k)
    pos = [[p[0] * k + p[1] if p else k * k + r] * width for r, p in enumerate(pairs)]
    pad = [[0.0 if p else NEG_INF] * width for p in pairs]
    return jnp.asarray(pos, I32), jnp.asarray(pad, F32)


def _pair_combine(a, b, op):
    t = a.shape[1]
    bc = lambda x, i, n: jnp.broadcast_to(x[i:i + 1, :], (n, t))
    r = lax.broadcasted_iota(I32, (V7X_SUBLANES, t), 0)
    ea = jnp.where(r < 3, bc(a, 2, 8), jnp.where(r < 5, bc(a, 3, 8), bc(a, 4, 8)))
    eb = jnp.where((r == 0) | (r == 3) | (r == 5), bc(b, 2, 8), jnp.where((r == 1) | (r == 4), bc(b, 3, 8), bc(b, 4, 8)))
    return jnp.concatenate([op(bc(a, 0, 16), b), op(bc(a, 1, 8), b[:8]), op(a, bc(b, 0, 16)),
                            op(a[:8], bc(b, 1, 8)), op(ea, eb)], axis=0)


def _topk_rows(x, k, order=None, payload=None):
    n, t = x.shape
    if order is None:
        order = lax.broadcasted_iota(I32, (n, t), 0)
    slot = lax.broadcasted_iota(I32, (k, t), 0)
    big = jnp.iinfo(jnp.int32).max

    def step(i, carry):
        x, vals, picks = carry
        m = jnp.max(x, axis=0, keepdims=True)
        first = jnp.min(jnp.where(x == m, order, big), axis=0, keepdims=True)
        hit = order == first
        pick = first if payload is None else jnp.max(jnp.where(hit, payload, -1), axis=0, keepdims=True)
        vals = jnp.where(slot == i, m, vals)
        picks = jnp.where(slot == i, pick, picks)
        return jnp.where(hit, NEG_INF, x), vals, picks

    _, vals, picks = lax.fori_loop(0, k, step, (x, jnp.zeros((k, t), F32), jnp.zeros((k, t), I32)))
    return vals, picks


def _peer_route_kernel(qp_ref, keys_ref, pos_ref, pad_ref, e_ref, g_ref, sv_ref, si_ref):
    tm = qp_ref.shape[0]

    def head(h, carry):
        s = []
        for p in range(2):
            c0 = pl.multiple_of((h * 2 + p) * HALF_KEY, HALF_KEY)
            q_hp = qp_ref[:, pl.ds(c0, HALF_KEY)].astype(BF16)
            s.append(lax.dot_general(keys_ref[p], q_hp, NT_DIMS, preferred_element_type=F32))
        sv_ref[h], si_ref[h] = _topk_rows(jnp.concatenate(s, axis=1), TOPK)
        return carry

    lax.fori_loop(0, PEER_HEADS, head, 0)
    for h0 in range(0, PEER_HEADS, ROUTE_GROUP):
        heads = range(h0, h0 + ROUTE_GROUP)
        side = lambda ref, p: jnp.concatenate([ref[h][:, p * tm:(p + 1) * tm] for h in heads], axis=1)
        comb = _pair_combine(side(sv_ref, 0), side(sv_ref, 1), lambda x, y: x + y) + pad_ref[...]
        cidx = _pair_combine(side(si_ref, 0), side(si_ref, 1), lambda x, y: x * N_KEYS + y)
        tv, eidx = _topk_rows(comb, TOPK, order=pos_ref[...], payload=cidx)
        ex = jnp.exp(tv - jnp.max(tv, axis=0, keepdims=True))
        gate = ex / jnp.sum(ex, axis=0, keepdims=True)
        for n, h in enumerate(heads):
            e_ref[h * TOPK:(h + 1) * TOPK, :] = eidx[:, n * tm:(n + 1) * tm] * HALF_ROWS
            g_ref[h * TOPK:(h + 1) * TOPK, :] = gate[:, n * tm:(n + 1) * tm]


def _peer_route(qp, keys):
    t = qp.shape[0]
    tm = V7X_LANES
    rows = PEER_HEADS * TOPK
    tables = _candidate_tables(ROUTE_GROUP * tm)
    out = pl.BlockSpec((rows, tm), lambda i: (0, i))
    return pl.pallas_call(
        _peer_route_kernel,
        grid=(t // tm,),
        in_specs=[pl.BlockSpec((tm, qp.shape[1]), lambda i: (i, 0)), _const_spec(keys.shape)]
                 + [_const_spec(a.shape) for a in tables],
        out_specs=(out, out),
        out_shape=(jax.ShapeDtypeStruct((rows, t), I32), jax.ShapeDtypeStruct((rows, t), F32)),
        scratch_shapes=[pltpu.VMEM((PEER_HEADS, TOPK, 2 * tm), F32), pltpu.VMEM((PEER_HEADS, TOPK, 2 * tm), I32)],
        compiler_params=_cparams("arbitrary"),
        name="peer_route",
    )(qp, keys, *tables)


PAIRS = PEER_HEADS * TOPK
HALF_ROWS = D_MODEL // 2 // V7X_LANES
HI_MASK = -65536


def _pack_expert_table(tab):
    bits = lax.bitcast_convert_type(tab.astype(BF16), jnp.uint16).astype(jnp.uint32)
    word = bits[:, :D_MODEL // 2] | (bits[:, D_MODEL // 2:] << 16)
    flat = lax.bitcast_convert_type(word, I32).reshape(tab.shape[0] * HALF_ROWS, V7X_LANES)
    return jnp.pad(flat, ((0, V7X_SUBLANES - HALF_ROWS), (0, 0)))


def _expert_halves(tab_ref, off):
    w = tab_ref[pl.ds(off, V7X_SUBLANES), :]
    return pltpu.bitcast(w << 16, F32), pltpu.bitcast(w & HI_MASK, F32)


def _split_halves(x):
    t = x.shape[0]
    return jnp.pad(x.reshape(t, 2, HALF_ROWS, V7X_LANES), ((0, 0), (0, 0), (0, V7X_SUBLANES - HALF_ROWS), (0, 0)))


def _pair_row_sums(ps):
    sub = lax.broadcasted_iota(I32, ps[0].shape, 0)
    stack = lambda a, b: ps[a] + pltpu.roll(ps[b], HALF_ROWS, 0)
    level = [stack(0, 4), stack(2, 6), stack(1, 5), stack(3, 7)]
    for h in (2, 1):
        first = (sub & (2 * h - 1)) < h
        level = [jnp.where(first, x, pltpu.roll(y, h, 0)) + jnp.where(first, pltpu.roll(x, V7X_SUBLANES - h, 0), y)
                 for x, y in zip(level[0::2], level[1::2])]
    return level[0]


def _peer_dot_kernel(off_ref, x_ref, g_ref, tab_ref, w_ref, r_ref, a_ref, *, tb):
    def token(t, carry):
        x_lo = x_ref[t, 0]
        x_hi = x_ref[t, 1]
        offs = off_ref.at[t]
        base = pl.multiple_of(t * PAIRS, PAIRS)
        for grp in range(PAIRS // V7X_SUBLANES):
            ps = []
            for j in range(V7X_SUBLANES):
                lo, hi = _expert_halves(tab_ref, offs[grp * V7X_SUBLANES + j])
                ps.append(lo * x_lo + hi * x_hi)
            r_ref[pl.ds(base + grp * V7X_SUBLANES, V7X_SUBLANES), :] = _pair_row_sums(ps)
        return carry

    lax.fori_loop(0, tb, token, 0)
    ones = jnp.ones((V7X_SUBLANES, V7X_LANES), BF16)
    r = r_ref[...]
    hi = r.astype(BF16)
    lo = (r - hi.astype(F32)).astype(BF16)
    sums = (lax.dot_general(ones, hi, NT_DIMS, preferred_element_type=F32)
            + lax.dot_general(ones, lo, NT_DIMS, preferred_element_type=F32))
    for t in range(tb):
        a_ref[t:t + 1, :] = sums[0:1, t * PAIRS:(t + 1) * PAIRS]
    a = a_ref[...]
    gelu = a * (lax.erf(a * (2.0 ** -0.5)) + 1.0) * 0.5
    w_ref[...] = g_ref[...] * gelu


def _peer_dot(off, x_halves, g, tab, *, tb):
    t = off.shape[0]
    return pl.pallas_call(
        functools.partial(_peer_dot_kernel, tb=tb),
        grid=(t // tb,),
        in_specs=[pl.BlockSpec((tb, PAIRS), lambda i: (i, 0), memory_space=pltpu.SMEM),
                  pl.BlockSpec((tb, 2, V7X_SUBLANES, V7X_LANES), lambda i: (i, 0, 0, 0)),
                  pl.BlockSpec((tb, PAIRS), lambda i: (i, 0)),
                  _const_spec(tab.shape)],
        out_specs=pl.BlockSpec((tb, PAIRS), lambda i: (i, 0)),
        out_shape=jax.ShapeDtypeStruct((t, PAIRS), F32),
        scratch_shapes=[pltpu.VMEM((tb * PAIRS, V7X_LANES), F32), pltpu.VMEM((tb, PAIRS), F32)],
        compiler_params=_cparams("arbitrary"),
        name="peer_dot",
    )(off, x_halves, g, tab)


def _peer_sum_kernel(off_ref, w_ref, tab_ref, y_ref, *, tb):
    n_acc = 2

    def token(t, carry):
        zero = jnp.zeros((V7X_SUBLANES, V7X_LANES), F32)
        acc_lo, acc_hi = [zero] * n_acc, [zero] * n_acc
        offs = off_ref.at[t]
        ws = w_ref.at[t]
        for k in range(PAIRS):
            lo, hi = _expert_halves(tab_ref, offs[k])
            acc_lo[k % n_acc] = acc_lo[k % n_acc] + ws[k] * lo
            acc_hi[k % n_acc] = acc_hi[k % n_acc] + ws[k] * hi
        y_ref[t, 0] = (acc_lo[0] + acc_lo[1])[:HALF_ROWS]
        y_ref[t, 1] = (acc_hi[0] + acc_hi[1])[:HALF_ROWS]
        return carry

    lax.fori_loop(0, tb, token, 0)


def _peer_sum(off, w, tab, *, tb):
    t = off.shape[0]
    smem = pl.BlockSpec((tb, PAIRS), lambda i: (i, 0), memory_space=pltpu.SMEM)
    return pl.pallas_call(
        functools.partial(_peer_sum_kernel, tb=tb),
        grid=(t // tb,),
        in_specs=[smem, smem, _const_spec(tab.shape)],
        out_specs=pl.BlockSpec((tb, 2, HALF_ROWS, V7X_LANES), lambda i: (i, 0, 0, 0)),
        out_shape=jax.ShapeDtypeStruct((t, 2, HALF_ROWS, V7X_LANES), F32),
        compiler_params=_cparams("arbitrary"),
        name="peer_sum",
    )(off, w, tab)


def _ple_kernel(h_ref, y_ref, p_ref, gple_ref, wple_ref, wgate_ref, gfin_ref, o_ref):
    h = h_ref[...] + y_ref[...]
    gate = jax.nn.sigmoid(jnp.dot(_rms(h, gple_ref[...]).astype(BF16), wgate_ref[...], preferred_element_type=F32))
    h = h + jnp.dot(p_ref[...].astype(BF16), wple_ref[...], preferred_element_type=F32) * gate
    o_ref[...] = _rms(h, gfin_ref[...])


def _ple(h, y, p, g_ple, wple, wgate, g_final, *, tm):
    t = h.shape[0]
    row = lambda w: pl.BlockSpec((tm, w), lambda i: (i, 0))
    return pl.pallas_call(
        _ple_kernel,
        grid=(t // tm,),
        in_specs=[row(D_MODEL), row(D_MODEL), row(p.shape[1]), _const_spec(g_ple.shape),
                  _const_spec(wple.shape), _const_spec(wgate.shape), _const_spec(g_final.shape)],
        out_specs=row(D_MODEL),
        out_shape=jax.ShapeDtypeStruct((t, D_MODEL), F32),
        compiler_params=_cparams("arbitrary"),
        name="ple_out",
    )(h, y, p, g_ple, wple, wgate, g_final)


def _trunk(x, o, ydw, p, wts, *, tm, tb):
    h, xb, qp = _mix(x, o, ydw, wts["g_mix"], wts["wgate"], wts["watt"], wts["ln_g"], wts["ln_b"],
                     wts["wconv"], wts["wout"], wts["g_ffn"], wts["wpq"], tm=tm)
    e_t, g_t = _peer_route(qp, wts["keys"])
    off = e_t.T
    w = _peer_dot(off, _split_halves(xb), g_t.T, wts["tab_u"], tb=tb)
    y = _peer_sum(off, w, wts["tab_v"], tb=tb).reshape(x.shape[0], D_MODEL)
    return _ple(h, y, p, wts["g_ple"], wts["wple"], wts["wplegate"], wts["g_final"], tm=tm)


def kernel(x_prompt, x_sample, cache_k, cache_v, cache_lf, state_conv, page_table, p_prompt, p_sample,
           g_mix, w_in, b_f, w_dw, b_dw, ln_g, ln_b, w_conv_proj, w_att_proj, w_out, g_ffn, w_pq,
           sub_keys, exp_u, exp_v, g_ple, w_ple, w_ple_gate, g_final):
    assert g_mix.shape[0] == 1, "single-layer trunk"
    b, s, _ = x_prompt.shape
    bd, sd, _ = x_sample.shape
    tp, ts = b * s, bd * sd
    page = cache_k.shape[2]

    w = w_in[0]
    c_f, c_glu, c_gate = 3 * ATT_WIDTH, 3 * ATT_WIDTH + N_HEADS, 3 * ATT_WIDTH + N_HEADS + 2 * CONV_CH
    wqkv = w[:, :c_f].astype(BF16)
    wf_t = w[:, c_f:c_glu].T.astype(BF16)
    wglu = w[:, c_glu:c_gate].astype(BF16)
    row = lambda a: a.reshape(1, -1)
    w_dw_pad = jnp.pad(w_dw[0], ((0, 1), (0, 0)))
    wts = dict(
        g_mix=row(g_mix[0]), wgate=w[:, c_gate:].astype(BF16), watt=w_att_proj[0].astype(BF16),
        ln_g=row(ln_g[0]), ln_b=row(ln_b[0]), wconv=w_conv_proj[0].astype(BF16), wout=w_out[0].astype(BF16),
        g_ffn=row(g_ffn[0]), wpq=w_pq[0].astype(BF16), keys=sub_keys[0].astype(BF16),
        tab_u=_pack_expert_table(exp_u[0]), tab_v=_pack_expert_table(exp_v[0]),
        g_ple=row(g_ple[0]), wple=w_ple[0].astype(BF16), wplegate=w_ple_gate[0].astype(BF16),
        g_final=row(g_final))
    b_f_col = b_f[0].reshape(N_HEADS, 1)
    b_dw_row = row(b_dw[0])

    xp = x_prompt.reshape(tp, D_MODEL)
    q, k, v, kb, vb, lf, cum, glu = _in_projection(xp, wts["g_mix"], wqkv, wf_t, b_f_col, wglu, seg=s, tm=512,
                                                   q_scale=LOG2E * HEAD_DIM ** -0.5)
    o = _fox_prompt(q, kb, vb, cum.reshape(N_HEADS // 2, 2, tp), batch=b, seq=s, tq=512)
    ydw = _dwconv_prompt(glu, w_dw_pad, b_dw_row, seq=s, tm=512)
    y_prompt = _trunk(xp, o, ydw, p_prompt[0].reshape(tp, -1), wts, tm=256, tb=64).reshape(b, s, D_MODEL)
    k_prompt = k.reshape(1, b, s, N_HEADS, HEAD_DIM)
    v_prompt = v.reshape(1, b, s, N_HEADS, HEAD_DIM)
    lf_prompt = lf.T.reshape(1, b, s, N_HEADS)
    conv_prompt = glu.reshape(b, s, CONV_CH)[None, :, s - (CONV_WIDTH - 1):, :]

    xs = x_sample.reshape(ts, D_MODEL)
    q, k, v, kb, vb, lf, cum, glu = _in_projection(xs, wts["g_mix"], wqkv, wf_t, b_f_col, wglu, seg=sd, tm=ts,
                                                   q_scale=HEAD_DIM ** -0.5)
    page_t = lambda a: jnp.swapaxes(jnp.pad(a.reshape(bd, sd, ATT_WIDTH), ((0, 0), (0, page - sd), (0, 0))), 1, 2)
    cn = jnp.pad(cum.reshape(N_HEADS, bd, sd).transpose(1, 0, 2), ((0, 0), (0, 0), (0, page - sd)))
    o = _fox_sample(q.astype(F32).reshape(bd, sd, ATT_WIDTH), page_t(kb), page_t(vb), cn,
                    jnp.transpose(cache_k[0], (0, 2, 3, 1)), jnp.transpose(cache_v[0], (0, 2, 3, 1)),
                    jnp.swapaxes(cache_lf[0], 1, 2), page_table, pp=8).reshape(ts, ATT_WIDTH)
    glu_s = glu.reshape(bd, sd, CONV_CH)
    ydw = _dwconv_sample(jnp.swapaxes(state_conv[0], 0, 1), jnp.swapaxes(glu_s, 0, 1), w_dw_pad, b_dw_row)
    ydw = jnp.swapaxes(ydw, 0, 1).reshape(ts, CONV_CH)
    y_sample = _trunk(xs, o, ydw, p_sample[0].reshape(ts, -1), wts, tm=ts, tb=64).reshape(bd, sd, D_MODEL)
    k_sample = k.reshape(1, bd, sd, N_HEADS, HEAD_DIM)
    v_sample = v.reshape(1, bd, sd, N_HEADS, HEAD_DIM)
    lf_sample = lf.T.reshape(1, bd, sd, N_HEADS)
    conv_sample = jnp.concatenate([state_conv[0][:, sd:], glu_s], axis=1)[None]

    return (y_prompt, y_sample, k_prompt, v_prompt, lf_prompt, conv_prompt,
            k_sample, v_sample, lf_sample, conv_sample)
```

```python
import functools

import jax
import jax.numpy as jnp
from jax import lax
from jax.experimental import pallas as pl
from jax.experimental.pallas import tpu as pltpu

F32 = jnp.float32
BF16 = jnp.bfloat16
I32 = jnp.int32

D_MODEL = 1024
N_HEADS = 8
HEAD_DIM = 64
ATT_WIDTH = N_HEADS * HEAD_DIM
CONV_CH = D_MODEL // 2
CONV_WIDTH = 31
PEER_HEADS = 8
N_KEYS = 128
HALF_KEY = 128
TOPK = 16
N_EXPERTS = N_KEYS * N_KEYS
RMS_EPS = 1e-6
LN_EPS = 1e-5

V7X_LANES = 128
V7X_SUBLANES = 8
V7X_VMEM_LIMIT_BYTES = 56 * 1024 * 1024

NEG_INF = float("-inf")
HIGHEST = lax.Precision.HIGHEST
NT_DIMS = (((1,), (1,)), ((), ()))


def _cparams(*sem):
    return pltpu.CompilerParams(dimension_semantics=sem, vmem_limit_bytes=V7X_VMEM_LIMIT_BYTES)


def _const_spec(shape):
    nd = len(shape)
    return pl.BlockSpec(shape, lambda *_: (0,) * nd, pipeline_mode=pl.Buffered(1))


def _rms(x, g):
    return x * lax.rsqrt(jnp.mean(x * x, axis=-1, keepdims=True) + RMS_EPS) * g


def _log_sigmoid(z):
    return jnp.minimum(z, 0.0) - jnp.log1p(jnp.exp(-jnp.abs(z)))


def _inproj_kernel(x_ref, g_ref, wqkv_ref, wf_ref, bf_ref, wglu_ref,
                   q_ref, k_ref, v_ref, kb_ref, vb_ref, lf_ref, cum_ref, glu_ref, carry_ref,
                   *, tm, seg, q_scale):
    i = pl.program_id(0)
    u = _rms(x_ref[...], g_ref[...]).astype(BF16)
    qkv = jnp.dot(u, wqkv_ref[...], preferred_element_type=F32)
    q_ref[...] = (qkv[:, :ATT_WIDTH] * q_scale).astype(BF16)
    k = qkv[:, ATT_WIDTH:2 * ATT_WIDTH]
    v = qkv[:, 2 * ATT_WIDTH:]
    k_ref[...] = k
    v_ref[...] = v
    kb_ref[...] = k.astype(BF16)
    vb_ref[...] = v.astype(BF16)
    cab = jnp.dot(u, wglu_ref[...], preferred_element_type=F32)
    glu_ref[...] = cab[:, :CONV_CH] * jax.nn.sigmoid(cab[:, CONV_CH:])
    fl = lax.dot_general(wf_ref[...], u, NT_DIMS, preferred_element_type=F32)
    lf = _log_sigmoid(fl + bf_ref[...])
    lf_ref[...] = lf
    src = lax.broadcasted_iota(I32, (tm, tm), 0)
    dst = lax.broadcasted_iota(I32, (tm, tm), 1)
    keep = src <= dst
    if seg < tm:
        keep = keep & ((src // seg) == (dst // seg))
    tri = jnp.where(keep, 1.0, 0.0).astype(F32)
    cum = jnp.dot(lf, tri, preferred_element_type=F32, precision=HIGHEST)
    if seg > tm:
        @pl.when(i % (seg // tm) == 0)
        def _():
            carry_ref[...] = jnp.zeros_like(carry_ref)
        cum = cum + carry_ref[:, 0:1]
        carry_ref[...] = jnp.broadcast_to(cum[:, tm - 1:tm], carry_ref.shape)
    cum_ref[...] = cum


def _in_projection(x, g_mix, wqkv, wf_t, b_f, wglu, *, seg, tm, q_scale):
    t = x.shape[0]
    row = lambda w: pl.BlockSpec((tm, w), lambda i: (i, 0))
    col = pl.BlockSpec((N_HEADS, tm), lambda i: (0, i))
    out_shape = (
        jax.ShapeDtypeStruct((t, ATT_WIDTH), BF16),
        jax.ShapeDtypeStruct((t, ATT_WIDTH), F32),
        jax.ShapeDtypeStruct((t, ATT_WIDTH), F32),
        jax.ShapeDtypeStruct((t, ATT_WIDTH), BF16),
        jax.ShapeDtypeStruct((t, ATT_WIDTH), BF16),
        jax.ShapeDtypeStruct((N_HEADS, t), F32),
        jax.ShapeDtypeStruct((N_HEADS, t), F32),
        jax.ShapeDtypeStruct((t, CONV_CH), F32),
    )
    return pl.pallas_call(
        functools.partial(_inproj_kernel, tm=tm, seg=seg, q_scale=q_scale),
        grid=(t // tm,),
        in_specs=[row(D_MODEL), _const_spec((1, D_MODEL)), _const_spec(wqkv.shape),
                  _const_spec(wf_t.shape), _const_spec((N_HEADS, 1)), _const_spec(wglu.shape)],
        out_specs=(row(ATT_WIDTH), row(ATT_WIDTH), row(ATT_WIDTH), row(ATT_WIDTH), row(ATT_WIDTH),
                   col, col, row(CONV_CH)),
        out_shape=out_shape,
        scratch_shapes=[pltpu.VMEM((N_HEADS, V7X_LANES), F32)],
        compiler_params=_cparams("arbitrary"),
        name="in_projection",
    )(x, g_mix, wqkv, wf_t, b_f, wglu)


LOG2E = 1.4426950408889634
BIAS_PIECES = 3


def _fox_prompt_kernel(q_ref, k_ref, v_ref, cum_ref, o_ref, kaug_ref, vt_ref, m_ref, l_ref, acc_ref, *, tq):
    qi = pl.program_id(2)
    seq = k_ref.shape[0]
    lane_k = lax.broadcasted_iota(I32, (seq, V7X_LANES), 1)

    @pl.when(qi == 0)
    def _():
        vt_ref[...] = v_ref[...].astype(F32).T.astype(BF16)
        k = k_ref[...]
        for h in range(2):
            c = cum_ref[h:h + 1, :] * LOG2E
            pieces = []
            for _ in range(BIAS_PIECES):
                piece = c.astype(BF16).astype(F32)
                pieces.append(piece)
                c = c - piece
            rows = jnp.concatenate(pieces + [jnp.zeros((V7X_SUBLANES - BIAS_PIECES, seq), F32)], axis=0)
            spare = (1 - h) * HEAD_DIM
            place = jnp.where((lax.broadcasted_iota(I32, (V7X_SUBLANES, V7X_LANES), 1) - spare)
                              == lax.broadcasted_iota(I32, (V7X_SUBLANES, V7X_LANES), 0), 1.0, 0.0)
            place = jnp.where(lax.broadcasted_iota(I32, place.shape, 0) < BIAS_PIECES, place, 0.0)
            extra = lax.dot_general(rows, place, (((0,), (0,)), ((), ())), preferred_element_type=F32)
            own = (lane_k >= h * HEAD_DIM) & (lane_k < (h + 1) * HEAD_DIM)
            kaug_ref[h] = jnp.where(own, k, extra.astype(BF16))

    q = q_ref[...]
    lane_q = lax.broadcasted_iota(I32, q.shape, 1)
    q_aug = []
    for h in range(2):
        own = (lane_q >= h * HEAD_DIM) & (lane_q < (h + 1) * HEAD_DIM)
        spare = (1 - h) * HEAD_DIM
        minus_one = (lane_q >= spare) & (lane_q < spare + BIAS_PIECES)
        q_aug.append(jnp.where(own, q, jnp.where(minus_one, -1.0, 0.0).astype(q.dtype)))
    m_ref[...] = jnp.full(m_ref.shape, NEG_INF, F32)
    l_ref[...] = jnp.zeros(l_ref.shape, F32)
    acc_ref[...] = jnp.zeros(acc_ref.shape, F32)

    def tile(kj, diagonal):
        ks = pl.multiple_of(kj * tq, tq)
        vt = vt_ref[:, pl.ds(ks, tq)]
        for h in range(2):
            st = lax.dot_general(kaug_ref[h, pl.ds(ks, tq), :], q_aug[h], NT_DIMS,
                                 preferred_element_type=F32)
            if diagonal:
                key = lax.broadcasted_iota(I32, st.shape, 0)
                qry = lax.broadcasted_iota(I32, st.shape, 1)
                st = jnp.where(key <= qry, st, NEG_INF)
            m_old = m_ref[h]
            m_new = jnp.maximum(m_old, jnp.max(st, axis=0, keepdims=True))
            alpha = jnp.exp2(m_old - m_new)
            p = jnp.exp2(st - m_new)
            l_ref[h] = alpha * l_ref[h] + jnp.sum(p, axis=0, keepdims=True)
            acc_ref[h] = alpha * acc_ref[h] + jnp.dot(vt, p.astype(BF16), preferred_element_type=F32)
            m_ref[h] = m_new

    def body(kj, carry):
        tile(kj, False)
        return carry

    lax.fori_loop(0, qi, body, 0)
    tile(qi, True)
    o0 = acc_ref[0] / l_ref[0]
    o1 = acc_ref[1] / l_ref[1]
    first_head = lax.broadcasted_iota(I32, o0.shape, 0) < HEAD_DIM
    o_ref[...] = jnp.where(first_head, o0, o1).T.astype(o_ref.dtype)


def _fox_prompt(q, kb, vb, cum_pairs, *, batch, seq, tq):
    nq = seq // tq
    pairs = N_HEADS // 2
    return pl.pallas_call(
        functools.partial(_fox_prompt_kernel, tq=tq),
        grid=(batch, pairs, nq),
        in_specs=[
            pl.BlockSpec((tq, V7X_LANES), lambda b, p, i: (b * nq + i, p)),
            pl.BlockSpec((seq, V7X_LANES), lambda b, p, i: (b, p)),
            pl.BlockSpec((seq, V7X_LANES), lambda b, p, i: (b, p)),
            pl.BlockSpec((None, 2, seq), lambda b, p, i: (p, 0, b)),
        ],
        out_specs=pl.BlockSpec((tq, V7X_LANES), lambda b, p, i: (b * nq + i, p)),
        out_shape=jax.ShapeDtypeStruct(q.shape, BF16),
        scratch_shapes=[pltpu.VMEM((2, seq, V7X_LANES), BF16), pltpu.VMEM((V7X_LANES, seq), BF16),
                        pltpu.VMEM((2, 1, tq), F32), pltpu.VMEM((2, 1, tq), F32),
                        pltpu.VMEM((2, V7X_LANES, tq), F32)],
        compiler_params=_cparams("arbitrary", "arbitrary", "arbitrary"),
        name="fox_prompt",
    )(q, kb, vb, cum_pairs)


def _fox_sample_kernel(pt_ref, q_ref, kn_ref, vn_ref, cn_ref, *rest, n_steps, pp, n_new):
    del pt_ref
    k_refs, v_refs, lf_refs = rest[:pp], rest[pp:2 * pp], rest[2 * pp:3 * pp]
    o_ref, qbd_ref, m_ref, l_ref, acc_ref, suf_ref = rest[3 * pp:]
    j = pl.program_id(1)
    rows = n_new * N_HEADS
    page = suf_ref.shape[1]
    own_head = (lax.broadcasted_iota(I32, (N_HEADS, ATT_WIDTH), 1) // HEAD_DIM
                == lax.broadcasted_iota(I32, (N_HEADS, ATT_WIDTH), 0))

    def update(s, vt):
        m_old = m_ref[...]
        m_new = jnp.maximum(m_old, jnp.max(s, axis=1, keepdims=True))
        alpha = jnp.exp(m_old - m_new)
        p = jnp.exp(s - m_new)
        l_ref[...] = alpha * l_ref[...] + jnp.sum(p, axis=1, keepdims=True)
        acc_ref[...] = alpha * acc_ref[...] + lax.dot_general(p.astype(BF16), vt, NT_DIMS,
                                                              preferred_element_type=F32)
        m_ref[...] = m_new

    @pl.when(j == 0)
    def _():
        q = q_ref[...]
        qbd = jnp.concatenate(
            [jnp.where(own_head, jnp.broadcast_to(q[t:t + 1, :], own_head.shape), 0.0) for t in range(n_new)],
            axis=0)
        qbd_ref[...] = qbd.astype(BF16)
        m_ref[...] = jnp.full(m_ref.shape, NEG_INF, F32)
        l_ref[...] = jnp.zeros(l_ref.shape, F32)
        acc_ref[...] = jnp.zeros(acc_ref.shape, F32)
        suf_ref[...] = jnp.zeros(suf_ref.shape, F32)
        cn = jnp.concatenate([cn_ref[...]] * n_new, axis=0)
        tok = lax.broadcasted_iota(I32, (rows, page), 0) // N_HEADS
        col = lax.broadcasted_iota(I32, (rows, page), 1)
        s = jnp.dot(qbd_ref[...], kn_ref[...], preferred_element_type=F32)
        update(jnp.where(col <= tok, s - cn, NEG_INF), vn_ref[...])

    @pl.when(j > 0)
    def _():
        lane = lax.broadcasted_iota(I32, suf_ref.shape, 1)
        kts, vts, biases = [], [], []
        suffix = suf_ref[...]
        for i in range(pp):
            lf = lf_refs[i][...]
            inc = lf
            shift = 1
            while shift < page:
                inc = inc + jnp.where(lane < page - shift, pltpu.roll(inc, page - shift, 1), 0.0)
                shift *= 2
            biases.append(jnp.concatenate([(inc - lf) + suffix] * n_new, axis=0))
            suffix = suffix + jnp.broadcast_to(inc[:, 0:1], inc.shape)
            kts.append(k_refs[i][...].reshape(ATT_WIDTH, page).astype(BF16))
            vts.append(v_refs[i][...].reshape(ATT_WIDTH, page).astype(BF16))
        suf_ref[...] = suffix
        s = jnp.dot(qbd_ref[...], jnp.concatenate(kts, axis=1), preferred_element_type=F32)
        update(s + jnp.concatenate(biases, axis=1), jnp.concatenate(vts, axis=1))

    @pl.when(j == n_steps - 1)
    def _():
        o = acc_ref[...] / l_ref[...]
        out = [jnp.sum(jnp.where(own_head, o[t * N_HEADS:(t + 1) * N_HEADS, :], 0.0), axis=0, keepdims=True)
               for t in range(n_new)]
        o_ref[...] = jnp.concatenate(out, axis=0)


def _fox_sample(q, kn_t, vn_t, cn, cache_kt, cache_vt, cache_lft, page_table, *, pp):
    bd, n_new, _ = q.shape
    n_pages = page_table.shape[1]
    page = cache_kt.shape[3]
    rows = n_new * N_HEADS
    n_steps = n_pages // pp + 1
    per_seq = lambda b, j, pt: (b, 0, 0)

    def pool_idx(i, nd):
        def index(b, j, pt):
            return (pt[b, n_pages - 1 - ((jnp.maximum(j, 1) - 1) * pp + i)],) + (0,) * nd
        return index

    kv_spec = lambda i: pl.BlockSpec((None, N_HEADS, HEAD_DIM, page), pool_idx(i, 3))
    lf_spec = lambda i: pl.BlockSpec((None, N_HEADS, page), pool_idx(i, 2))
    grid_spec = pltpu.PrefetchScalarGridSpec(
        num_scalar_prefetch=1,
        grid=(bd, n_steps),
        in_specs=[pl.BlockSpec((None, n_new, ATT_WIDTH), per_seq),
                  pl.BlockSpec((None, ATT_WIDTH, page), per_seq),
                  pl.BlockSpec((None, ATT_WIDTH, page), per_seq),
                  pl.BlockSpec((None, N_HEADS, page), per_seq)]
                 + [kv_spec(i) for i in range(pp)] + [kv_spec(i) for i in range(pp)]
                 + [lf_spec(i) for i in range(pp)],
        out_specs=pl.BlockSpec((None, n_new, ATT_WIDTH), per_seq),
        scratch_shapes=[pltpu.VMEM((rows, ATT_WIDTH), BF16), pltpu.VMEM((rows, 1), F32),
                        pltpu.VMEM((rows, 1), F32), pltpu.VMEM((rows, ATT_WIDTH), F32),
                        pltpu.VMEM((N_HEADS, page), F32)],
    )
    return pl.pallas_call(
        functools.partial(_fox_sample_kernel, n_steps=n_steps, pp=pp, n_new=n_new),
        grid_spec=grid_spec,
        out_shape=jax.ShapeDtypeStruct((bd, n_new, ATT_WIDTH), F32),
        compiler_params=_cparams("arbitrary", "arbitrary"),
        name="fox_sample",
    )(page_table, q, kn_t, vn_t, cn, *([cache_kt] * pp), *([cache_vt] * pp), *([cache_lft] * pp))


HALO = 32


def _dwconv_prompt_kernel(glu_ref, halo_ref, w_ref, b_ref, y_ref, cat_ref, *, tm, tiles_per_seq):
    i = pl.program_id(0)
    starts_sequence = (i % tiles_per_seq) == 0
    cat_ref[0:HALO, :] = jnp.where(starts_sequence, 0.0, halo_ref[...])
    cat_ref[HALO:HALO + tm, :] = glu_ref[...]
    off = HALO - (CONV_WIDTH - 1)
    acc = jnp.broadcast_to(b_ref[...], (tm, CONV_CH))
    for w in range(CONV_WIDTH):
        acc = acc + cat_ref[pl.ds(off + w, tm), :] * w_ref[w:w + 1, :]
    y_ref[...] = acc


def _dwconv_prompt(glu, w_dw, b_dw, *, seq, tm):
    t = glu.shape[0]
    per_tile = tm // HALO
    return pl.pallas_call(
        functools.partial(_dwconv_prompt_kernel, tm=tm, tiles_per_seq=seq // tm),
        grid=(t // tm,),
        in_specs=[pl.BlockSpec((tm, CONV_CH), lambda i: (i, 0)),
                  pl.BlockSpec((HALO, CONV_CH), lambda i: (jnp.maximum(i * per_tile - 1, 0), 0)),
                  _const_spec(w_dw.shape), _const_spec(b_dw.shape)],
        out_specs=pl.BlockSpec((tm, CONV_CH), lambda i: (i, 0)),
        out_shape=jax.ShapeDtypeStruct(glu.shape, F32),
        scratch_shapes=[pltpu.VMEM((HALO + tm, CONV_CH), F32)],
        compiler_params=_cparams("arbitrary"),
        name="dwconv_prompt",
    )(glu, glu, w_dw, b_dw)


def _dwconv_sample_kernel(hist_ref, new_ref, w_ref, b_ref, y_ref, *, n_new):
    hist = CONV_WIDTH - 1
    for t in range(n_new):
        acc = jnp.broadcast_to(b_ref[...], y_ref.shape[1:])
        for w in range(CONV_WIDTH):
            pos = t + w
            row = hist_ref[pos] if pos < hist else new_ref[pos - hist]
            acc = acc + row * w_ref[w:w + 1, :]
        y_ref[t] = acc


def _dwconv_sample(hist_tm, new_tm, w_dw, b_dw):
    n_new = new_tm.shape[0]
    return pl.pallas_call(
        functools.partial(_dwconv_sample_kernel, n_new=n_new),
        out_shape=jax.ShapeDtypeStruct(new_tm.shape, F32),
        name="dwconv_sample",
    )(hist_tm, new_tm, w_dw, b_dw)


def _mix_kernel(x_ref, o_ref, y_ref, gmix_ref, wgate_ref, watt_ref, lng_ref, lnb_ref, wconv_ref,
                wout_ref, gffn_ref, wpq_ref, h_ref, xb_ref, qp_ref):
    x = x_ref[...]
    y = y_ref[...]
    mu = jnp.mean(y, axis=-1, keepdims=True)
    var = jnp.mean(jnp.square(y - mu), axis=-1, keepdims=True)
    z = (y - mu) * lax.rsqrt(var + LN_EPS) * lng_ref[...] + lnb_ref[...]
    z = z * jax.nn.sigmoid(z)
    conv = jnp.dot(z.astype(BF16), wconv_ref[...], preferred_element_type=F32)
    att = jnp.dot(o_ref[...].astype(BF16), watt_ref[...], preferred_element_type=F32)
    u = _rms(x, gmix_ref[...]).astype(BF16)
    gates = jnp.dot(u, wgate_ref[...], preferred_element_type=F32)
    merged = jax.nn.sigmoid(gates[:, :D_MODEL]) * att + jax.nn.sigmoid(gates[:, D_MODEL:]) * conv
    h = x + jnp.dot(merged.astype(BF16), wout_ref[...], preferred_element_type=F32)
    h_ref[...] = h
    xb = _rms(h, gffn_ref[...])
    xb_ref[...] = xb
    qp_ref[...] = jnp.dot(xb.astype(BF16), wpq_ref[...], preferred_element_type=F32)


def _mix(x, o, ydw, g_mix, wgate, watt, ln_g, ln_b, wconv, wout, g_ffn, wpq, *, tm):
    t = x.shape[0]
    row = lambda w: pl.BlockSpec((tm, w), lambda i: (i, 0))
    nq = wpq.shape[1]
    return pl.pallas_call(
        _mix_kernel,
        grid=(t // tm,),
        in_specs=[row(D_MODEL), row(ATT_WIDTH), row(CONV_CH), _const_spec(g_mix.shape),
                  _const_spec(wgate.shape), _const_spec(watt.shape), _const_spec(ln_g.shape),
                  _const_spec(ln_b.shape), _const_spec(wconv.shape), _const_spec(wout.shape),
                  _const_spec(g_ffn.shape), _const_spec(wpq.shape)],
        out_specs=(row(D_MODEL), row(D_MODEL), row(nq)),
        out_shape=(jax.ShapeDtypeStruct((t, D_MODEL), F32), jax.ShapeDtypeStruct((t, D_MODEL), F32),
                   jax.ShapeDtypeStruct((t, nq), F32)),
        compiler_params=_cparams("arbitrary"),
        name="mix",
    )(x, o, ydw, g_mix, wgate, watt, ln_g, ln_b, wconv, wout, g_ffn, wpq)


CAND_ROWS = 56
ROUTE_GROUP = 2


def _candidate_tables(width):
    assert TOPK == 16
    k = TOPK
    pairs = ([(0, j) for j in range(16)] + [(1, j) for j in range(8)]
             + [(i, 0) if i >= 2 else None for i in range(16)] + [(i, 1) if i >= 2 else None for i in range(8)]
             + [(2, 2), (2, 3), (2, 4), (3, 2), (3, 3), (4, 2), None, None])
    assert len(pairs) == CAND_ROWS
    assert sorted(p for p in pairs if p) == sorted((i, j) for i in range(k) for j in range(k) if (i + 1) * (j + 1) <= k)
    pos = [[p[0] * k + p[1] if p else k * k + r] * width for r, p in enumerate(pairs)]
    pad = [[0.0 if p else NEG_INF] * width for p in pairs]
    return jnp.asarray(pos, I32), jnp.asarray(pad, F32)


def _pair_combine(a, b, op):
    t = a.shape[1]
    bc = lambda x, i, n: jnp.broadcast_to(x[i:i + 1, :], (n, t))
    r = lax.broadcasted_iota(I32, (V7X_SUBLANES, t), 0)
    ea = jnp.where(r < 3, bc(a, 2, 8), jnp.where(r < 5, bc(a, 3, 8), bc(a, 4, 8)))
    eb = jnp.where((r == 0) | (r == 3) | (r == 5), bc(b, 2, 8), jnp.where((r == 1) | (r == 4), bc(b, 3, 8), bc(b, 4, 8)))
    return jnp.concatenate([op(bc(a, 0, 16), b), op(bc(a, 1, 8), b[:8]), op(a, bc(b, 0, 16)),
                            op(a[:8], bc(b, 1, 8)), op(ea, eb)], axis=0)


def _topk_rows(x, k, order=None, payload=None):
    n, t = x.shape
    if order is None:
        order = lax.broadcasted_iota(I32, (n, t), 0)
    slot = lax.broadcasted_iota(I32, (k, t), 0)
    big = jnp.iinfo(jnp.int32).max

    def step(i, carry):
        x, vals, picks = carry
        m = jnp.max(x, axis=0, keepdims=True)
        first = jnp.min(jnp.where(x == m, order, big), axis=0, keepdims=True)
        hit = order == first
        pick = first if payload is None else jnp.max(jnp.where(hit, payload, -1), axis=0, keepdims=True)
        vals = jnp.where(slot == i, m, vals)
        picks = jnp.where(slot == i, pick, picks)
        return jnp.where(hit, NEG_INF, x), vals, picks

    _, vals, picks = lax.fori_loop(0, k, step, (x, jnp.zeros((k, t), F32), jnp.zeros((k, t), I32)))
    return vals, picks


def _peer_route_kernel(qp_ref, keys_ref, pos_ref, pad_ref, e_ref, g_ref, sv_ref, si_ref):
    tm = qp_ref.shape[0]

    def head(h, carry):
        s = []
        for p in range(2):
            c0 = pl.multiple_of((h * 2 + p) * HALF_KEY, HALF_KEY)
            q_hp = qp_ref[:, pl.ds(c0, HALF_KEY)].astype(BF16)
            s.append(lax.dot_general(keys_ref[p], q_hp, NT_DIMS, preferred_element_type=F32))
        sv_ref[h], si_ref[h] = _topk_rows(jnp.concatenate(s, axis=1), TOPK)
        return carry

    lax.fori_loop(0, PEER_HEADS, head, 0)
    for h0 in range(0, PEER_HEADS, ROUTE_GROUP):
        heads = range(h0, h0 + ROUTE_GROUP)
        side = lambda ref, p: jnp.concatenate([ref[h][:, p * tm:(p + 1) * tm] for h in heads], axis=1)
        comb = _pair_combine(side(sv_ref, 0), side(sv_ref, 1), lambda x, y: x + y) + pad_ref[...]
        cidx = _pair_combine(side(si_ref, 0), side(si_ref, 1), lambda x, y: x * N_KEYS + y)
        tv, eidx = _topk_rows(comb, TOPK, order=pos_ref[...], payload=cidx)
        ex = jnp.exp(tv - jnp.max(tv, axis=0, keepdims=True))
        gate = ex / jnp.sum(ex, axis=0, keepdims=True)
        for n, h in enumerate(heads):
            e_ref[h * TOPK:(h + 1) * TOPK, :] = eidx[:, n * tm:(n + 1) * tm] * HALF_ROWS
            g_ref[h * TOPK:(h + 1) * TOPK, :] = gate[:, n * tm:(n + 1) * tm]


def _peer_route(qp, keys):
    t = qp.shape[0]
    tm = V7X_LANES
    rows = PEER_HEADS * TOPK
    tables = _candidate_tables(ROUTE_GROUP * tm)
    out = pl.BlockSpec((rows, tm), lambda i: (0, i))
    return pl.pallas_call(
        _peer_route_kernel,
        grid=(t // tm,),
        in_specs=[pl.BlockSpec((tm, qp.shape[1]), lambda i: (i, 0)), _const_spec(keys.shape)]
                 + [_const_spec(a.shape) for a in tables],
        out_specs=(out, out),
        out_shape=(jax.ShapeDtypeStruct((rows, t), I32), jax.ShapeDtypeStruct((rows, t), F32)),
        scratch_shapes=[pltpu.VMEM((PEER_HEADS, TOPK, 2 * tm), F32), pltpu.VMEM((PEER_HEADS, TOPK, 2 * tm), I32)],
        compiler_params=_cparams("arbitrary"),
        name="peer_route",
    )(qp, keys, *tables)


PAIRS = PEER_HEADS * TOPK
HALF_ROWS = D_MODEL // 2 // V7X_LANES
HI_MASK = -65536


def _pack_expert_table(tab):
    bits = lax.bitcast_convert_type(tab.astype(BF16), jnp.uint16).astype(jnp.uint32)
    word = bits[:, :D_MODEL // 2] | (bits[:, D_MODEL // 2:] << 16)
    flat = lax.bitcast_convert_type(word, I32).reshape(tab.shape[0] * HALF_ROWS, V7X_LANES)
    return jnp.pad(flat, ((0, V7X_SUBLANES - HALF_ROWS), (0, 0)))


def _expert_halves(tab_ref, off):
    w = tab_ref[pl.ds(off, V7X_SUBLANES), :]
    return pltpu.bitcast(w << 16, F32), pltpu.bitcast(w & HI_MASK, F32)


def _row_halves(x):
    pad = jnp.zeros((V7X_SUBLANES - HALF_ROWS, V7X_LANES), F32)
    chunk = lambda c: x[:, c * V7X_LANES:(c + 1) * V7X_LANES]
    return (jnp.concatenate([chunk(r) for r in range(HALF_ROWS)] + [pad], axis=0),
            jnp.concatenate([chunk(HALF_ROWS + r) for r in range(HALF_ROWS)] + [pad], axis=0))


def _pair_row_sums(ps):
    sub = lax.broadcasted_iota(I32, ps[0].shape, 0)
    stack = lambda a, b: ps[a] + pltpu.roll(ps[b], HALF_ROWS, 0)
    level = [stack(0, 4), stack(2, 6), stack(1, 5), stack(3, 7)]
    for h in (2, 1):
        first = (sub & (2 * h - 1)) < h
        level = [jnp.where(first, x, pltpu.roll(y, h, 0)) + jnp.where(first, pltpu.roll(x, V7X_SUBLANES - h, 0), y)
                 for x, y in zip(level[0::2], level[1::2])]
    return level[0]


def _peer_dot_kernel(off_ref, x_ref, g_ref, tab_ref, w_ref, r_ref, a_ref, *, tb):
    def token(t, carry):
        x_lo, x_hi = _row_halves(x_ref[pl.ds(t, 1), :])
        offs = off_ref.at[t]
        base = pl.multiple_of(t * PAIRS, PAIRS)
        for grp in range(PAIRS // V7X_SUBLANES):
            ps = []
            for j in range(V7X_SUBLANES):
                lo, hi = _expert_halves(tab_ref, offs[j * TOPK + grp])
                ps.append(lo * x_lo + hi * x_hi)
            r_ref[pl.ds(base + grp * V7X_SUBLANES, V7X_SUBLANES), :] = _pair_row_sums(ps)
        return carry

    lax.fori_loop(0, tb, token, 0)
    ones = jnp.ones((V7X_SUBLANES, V7X_LANES), BF16)
    r = r_ref[...]
    hi = r.astype(BF16)
    lo = (r - hi.astype(F32)).astype(BF16)
    sums = (lax.dot_general(ones, hi, NT_DIMS, preferred_element_type=F32)
            + lax.dot_general(ones, lo, NT_DIMS, preferred_element_type=F32))
    for t in range(tb):
        a_ref[t:t + 1, :] = sums[0:1, t * PAIRS:(t + 1) * PAIRS]
    a = a_ref[...]
    gelu = a * (lax.erf(a * (2.0 ** -0.5)) + 1.0) * 0.5
    w_ref[...] = g_ref[...] * gelu


def _peer_dot(off, x, g, tab, *, tb):
    t = off.shape[0]
    return pl.pallas_call(
        functools.partial(_peer_dot_kernel, tb=tb),
        grid=(t // tb,),
        in_specs=[pl.BlockSpec((tb, PAIRS), lambda i: (i, 0), memory_space=pltpu.SMEM),
                  pl.BlockSpec((tb, D_MODEL), lambda i: (i, 0)),
                  pl.BlockSpec((tb, PAIRS), lambda i: (i, 0)),
                  _const_spec(tab.shape)],
        out_specs=pl.BlockSpec((tb, PAIRS), lambda i: (i, 0)),
        out_shape=jax.ShapeDtypeStruct((t, PAIRS), F32),
        scratch_shapes=[pltpu.VMEM((tb * PAIRS, V7X_LANES), F32), pltpu.VMEM((tb, PAIRS), F32)],
        compiler_params=_cparams("arbitrary"),
        name="peer_dot",
    )(off, x, g, tab)


def _peer_sum_kernel(off_ref, w_ref, tab_ref, y_ref, wb_ref, *, tb):
    n_acc = 2

    def spread(t):
        return jnp.broadcast_to(w_ref[pl.ds(t, 1), :], (PAIRS, PAIRS)).T

    wb_ref[...] = spread(0)

    def token_group(g, carry):
        t0 = pl.multiple_of(g * V7X_SUBLANES, V7X_SUBLANES)
        y_rows = y_ref.at[pl.ds(t0, V7X_SUBLANES), :]
        for j in range(V7X_SUBLANES):
            t = t0 + j
            zero = jnp.zeros((V7X_SUBLANES, V7X_LANES), F32)
            acc_lo, acc_hi = [zero] * n_acc, [zero] * n_acc
            offs = [off_ref.at[t, pl.ds(c * TOPK, TOPK)] for c in range(PEER_HEADS)]
            nxt = spread(jnp.minimum(t + 1, tb - 1))
            for n in range(PAIRS):
                lo, hi = _expert_halves(tab_ref, offs[n % PEER_HEADS][n // PEER_HEADS])
                wk = wb_ref[n:n + 1, :]
                acc_lo[n % n_acc] = acc_lo[n % n_acc] + wk * lo
                acc_hi[n % n_acc] = acc_hi[n % n_acc] + wk * hi
            for half, acc in enumerate((acc_lo[0] + acc_lo[1], acc_hi[0] + acc_hi[1])):
                for r in range(HALF_ROWS):
                    c = (half * HALF_ROWS + r) * V7X_LANES
                    y_rows[j:j + 1, c:c + V7X_LANES] = acc[r:r + 1, :]
            wb_ref[...] = nxt
        return carry

    lax.fori_loop(0, tb // V7X_SUBLANES, token_group, 0)


def _peer_sum(off, w, tab, *, tb):
    t = off.shape[0]
    return pl.pallas_call(
        functools.partial(_peer_sum_kernel, tb=tb),
        grid=(t // tb,),
        in_specs=[pl.BlockSpec((tb, PAIRS), lambda i: (i, 0), memory_space=pltpu.SMEM),
                  pl.BlockSpec((tb, PAIRS), lambda i: (i, 0)), _const_spec(tab.shape)],
        out_specs=pl.BlockSpec((tb, D_MODEL), lambda i: (i, 0)),
        out_shape=jax.ShapeDtypeStruct((t, D_MODEL), F32),
        scratch_shapes=[pltpu.VMEM((PAIRS, PAIRS), F32)],
        compiler_params=_cparams("arbitrary"),
        name="peer_sum",
    )(off, w, tab)


def _ple_kernel(h_ref, y_ref, p_ref, gple_ref, wple_ref, wgate_ref, gfin_ref, o_ref):
    h = h_ref[...] + y_ref[...]
    gate = jax.nn.sigmoid(jnp.dot(_rms(h, gple_ref[...]).astype(BF16), wgate_ref[...], preferred_element_type=F32))
    h = h + jnp.dot(p_ref[...].astype(BF16), wple_ref[...], preferred_element_type=F32) * gate
    o_ref[...] = _rms(h, gfin_ref[...])


def _ple(h, y, p, g_ple, wple, wgate, g_final, *, tm):
    t = h.shape[0]
    row = lambda w: pl.BlockSpec((tm, w), lambda i: (i, 0))
    return pl.pallas_call(
        _ple_kernel,
        grid=(t // tm,),
        in_specs=[row(D_MODEL), row(D_MODEL), row(p.shape[1]), _const_spec(g_ple.shape),
                  _const_spec(wple.shape), _const_spec(wgate.shape), _const_spec(g_final.shape)],
        out_specs=row(D_MODEL),
        out_shape=jax.ShapeDtypeStruct((t, D_MODEL), F32),
        compiler_params=_cparams("arbitrary"),
        name="ple_out",
    )(h, y, p, g_ple, wple, wgate, g_final)


def _trunk(x, o, ydw, p, wts, *, tm, tb):
    h, xb, qp = _mix(x, o, ydw, wts["g_mix"], wts["wgate"], wts["watt"], wts["ln_g"], wts["ln_b"],
                     wts["wconv"], wts["wout"], wts["g_ffn"], wts["wpq"], tm=tm)
    e_t, g_t = _peer_route(qp, wts["keys"])
    off = e_t.T
    g_visit = g_t.reshape(PEER_HEADS, TOPK, -1).transpose(2, 1, 0).reshape(-1, PAIRS)
    w = _peer_dot(off, xb, g_visit, wts["tab_u"], tb=tb)
    y = _peer_sum(off, w, wts["tab_v"], tb=tb)
    return _ple(h, y, p, wts["g_ple"], wts["wple"], wts["wplegate"], wts["g_final"], tm=tm)


def kernel(x_prompt, x_sample, cache_k, cache_v, cache_lf, state_conv, page_table, p_prompt, p_sample,
           g_mix, w_in, b_f, w_dw, b_dw, ln_g, ln_b, w_conv_proj, w_att_proj, w_out, g_ffn, w_pq,
           sub_keys, exp_u, exp_v, g_ple, w_ple, w_ple_gate, g_final):
    assert g_mix.shape[0] == 1, "single-layer trunk"
    b, s, _ = x_prompt.shape
    bd, sd, _ = x_sample.shape
    tp, ts = b * s, bd * sd
    page = cache_k.shape[2]

    w = w_in[0]
    c_f, c_glu, c_gate = 3 * ATT_WIDTH, 3 * ATT_WIDTH + N_HEADS, 3 * ATT_WIDTH + N_HEADS + 2 * CONV_CH
    wqkv = w[:, :c_f].astype(BF16)
    wf_t = w[:, c_f:c_glu].T.astype(BF16)
    wglu = w[:, c_glu:c_gate].astype(BF16)
    row = lambda a: a.reshape(1, -1)
    w_dw_pad = jnp.pad(w_dw[0], ((0, 1), (0, 0)))
    wts = dict(
        g_mix=row(g_mix[0]), wgate=w[:, c_gate:].astype(BF16), watt=w_att_proj[0].astype(BF16),
        ln_g=row(ln_g[0]), ln_b=row(ln_b[0]), wconv=w_conv_proj[0].astype(BF16), wout=w_out[0].astype(BF16),
        g_ffn=row(g_ffn[0]), wpq=w_pq[0].astype(BF16), keys=sub_keys[0].astype(BF16),
        tab_u=_pack_expert_table(exp_u[0]), tab_v=_pack_expert_table(exp_v[0]),
        g_ple=row(g_ple[0]), wple=w_ple[0].astype(BF16), wplegate=w_ple_gate[0].astype(BF16),
        g_final=row(g_final))
    b_f_col = b_f[0].reshape(N_HEADS, 1)
    b_dw_row = row(b_dw[0])

    xp = x_prompt.reshape(tp, D_MODEL)
    q, k, v, kb, vb, lf, cum, glu = _in_projection(xp, wts["g_mix"], wqkv, wf_t, b_f_col, wglu, seg=s, tm=512,
                                                   q_scale=LOG2E * HEAD_DIM ** -0.5)
    o = _fox_prompt(q, kb, vb, cum.reshape(N_HEADS // 2, 2, tp), batch=b, seq=s, tq=512)
    ydw = _dwconv_prompt(glu, w_dw_pad, b_dw_row, seq=s, tm=512)
    y_prompt = _trunk(xp, o, ydw, p_prompt[0].reshape(tp, -1), wts, tm=256, tb=64).reshape(b, s, D_MODEL)
    k_prompt = k.reshape(1, b, s, N_HEADS, HEAD_DIM)
    v_prompt = v.reshape(1, b, s, N_HEADS, HEAD_DIM)
    lf_prompt = lf.T.reshape(1, b, s, N_HEADS)
    conv_prompt = glu.reshape(b, s, CONV_CH)[None, :, s - (CONV_WIDTH - 1):, :]

    xs = x_sample.reshape(ts, D_MODEL)
    q, k, v, kb, vb, lf, cum, glu = _in_projection(xs, wts["g_mix"], wqkv, wf_t, b_f_col, wglu, seg=sd, tm=ts,
                                                   q_scale=HEAD_DIM ** -0.5)
    page_t = lambda a: jnp.swapaxes(jnp.pad(a.reshape(bd, sd, ATT_WIDTH), ((0, 0), (0, page - sd), (0, 0))), 1, 2)
    cn = jnp.pad(cum.reshape(N_HEADS, bd, sd).transpose(1, 0, 2), ((0, 0), (0, 0), (0, page - sd)))
    o = _fox_sample(q.astype(F32).reshape(bd, sd, ATT_WIDTH), page_t(kb), page_t(vb), cn,
                    jnp.transpose(cache_k[0], (0, 2, 3, 1)), jnp.transpose(cache_v[0], (0, 2, 3, 1)),
                    jnp.swapaxes(cache_lf[0], 1, 2), page_table, pp=8).reshape(ts, ATT_WIDTH)
    glu_s = glu.reshape(bd, sd, CONV_CH)
    ydw = _dwconv_sample(jnp.swapaxes(state_conv[0], 0, 1), jnp.swapaxes(glu_s, 0, 1), w_dw_pad, b_dw_row)
    ydw = jnp.swapaxes(ydw, 0, 1).reshape(ts, CONV_CH)
    y_sample = _trunk(xs, o, ydw, p_sample[0].reshape(ts, -1), wts, tm=ts, tb=64).reshape(bd, sd, D_MODEL)
    k_sample = k.reshape(1, bd, sd, N_HEADS, HEAD_DIM)
    v_sample = v.reshape(1, bd, sd, N_HEADS, HEAD_DIM)
    lf_sample = lf.T.reshape(1, bd, sd, N_HEADS)
    conv_sample = jnp.concatenate([state_conv[0][:, sd:], glu_s], axis=1)[None]

    return (y_prompt, y_sample, k_prompt, v_prompt, lf_prompt, conv_prompt,
            k_sample, v_sample, lf_sample, conv_sample)
```

```python
import functools

import jax
import jax.numpy as jnp
from jax import lax
from jax.experimental import pallas as pl
from jax.experimental.pallas import tpu as pltpu

F32 = jnp.float32
BF16 = jnp.bfloat16
I32 = jnp.int32

D_MODEL = 1024
N_HEADS = 8
HEAD_DIM = 64
ATT_WIDTH = N_HEADS * HEAD_DIM
CONV_CH = D_MODEL // 2
CONV_WIDTH = 31
PEER_HEADS = 8
N_KEYS = 128
HALF_KEY = 128
TOPK = 16
N_EXPERTS = N_KEYS * N_KEYS
RMS_EPS = 1e-6
LN_EPS = 1e-5

V7X_LANES = 128
V7X_SUBLANES = 8
V7X_VMEM_LIMIT_BYTES = 56 * 1024 * 1024

NEG_INF = float("-inf")
HIGHEST = lax.Precision.HIGHEST
NT_DIMS = (((1,), (1,)), ((), ()))


def _cparams(*sem):
    return pltpu.CompilerParams(dimension_semantics=sem, vmem_limit_bytes=V7X_VMEM_LIMIT_BYTES)


def _const_spec(shape):
    nd = len(shape)
    return pl.BlockSpec(shape, lambda *_: (0,) * nd, pipeline_mode=pl.Buffered(1))


def _rms(x, g):
    return x * lax.rsqrt(jnp.mean(x * x, axis=-1, keepdims=True) + RMS_EPS) * g


def _log_sigmoid(z):
    return jnp.minimum(z, 0.0) - jnp.log1p(jnp.exp(-jnp.abs(z)))


def _inproj_kernel(x_ref, g_ref, wqkv_ref, wf_ref, bf_ref, wglu_ref,
                   q_ref, k_ref, v_ref, kb_ref, vb_ref, lf_ref, cum_ref, glu_ref, carry_ref,
                   *, tm, seg, q_scale):
    i = pl.program_id(0)
    u = _rms(x_ref[...], g_ref[...]).astype(BF16)
    qkv = jnp.dot(u, wqkv_ref[...], preferred_element_type=F32)
    q_ref[...] = (qkv[:, :ATT_WIDTH] * q_scale).astype(BF16)
    k = qkv[:, ATT_WIDTH:2 * ATT_WIDTH]
    v = qkv[:, 2 * ATT_WIDTH:]
    k_ref[...] = k
    v_ref[...] = v
    kb_ref[...] = k.astype(BF16)
    vb_ref[...] = v.astype(BF16)
    cab = jnp.dot(u, wglu_ref[...], preferred_element_type=F32)
    glu_ref[...] = cab[:, :CONV_CH] * jax.nn.sigmoid(cab[:, CONV_CH:])
    fl = lax.dot_general(wf_ref[...], u, NT_DIMS, preferred_element_type=F32)
    lf = _log_sigmoid(fl + bf_ref[...])
    lf_ref[...] = lf
    src = lax.broadcasted_iota(I32, (tm, tm), 0)
    dst = lax.broadcasted_iota(I32, (tm, tm), 1)
    keep = src <= dst
    if seg < tm:
        keep = keep & ((src // seg) == (dst // seg))
    tri = jnp.where(keep, 1.0, 0.0).astype(F32)
    cum = jnp.dot(lf, tri, preferred_element_type=F32, precision=HIGHEST)
    if seg > tm:
        @pl.when(i % (seg // tm) == 0)
        def _():
            carry_ref[...] = jnp.zeros_like(carry_ref)
        cum = cum + carry_ref[:, 0:1]
        carry_ref[...] = jnp.broadcast_to(cum[:, tm - 1:tm], carry_ref.shape)
    cum_ref[...] = cum


def _in_projection(x, g_mix, wqkv, wf_t, b_f, wglu, *, seg, tm, q_scale):
    t = x.shape[0]
    row = lambda w: pl.BlockSpec((tm, w), lambda i: (i, 0))
    col = pl.BlockSpec((N_HEADS, tm), lambda i: (0, i))
    out_shape = (
        jax.ShapeDtypeStruct((t, ATT_WIDTH), BF16),
        jax.ShapeDtypeStruct((t, ATT_WIDTH), F32),
        jax.ShapeDtypeStruct((t, ATT_WIDTH), F32),
        jax.ShapeDtypeStruct((t, ATT_WIDTH), BF16),
        jax.ShapeDtypeStruct((t, ATT_WIDTH), BF16),
        jax.ShapeDtypeStruct((N_HEADS, t), F32),
        jax.ShapeDtypeStruct((N_HEADS, t), F32),
        jax.ShapeDtypeStruct((t, CONV_CH), F32),
    )
    return pl.pallas_call(
        functools.partial(_inproj_kernel, tm=tm, seg=seg, q_scale=q_scale),
        grid=(t // tm,),
        in_specs=[row(D_MODEL), _const_spec((1, D_MODEL)), _const_spec(wqkv.shape),
                  _const_spec(wf_t.shape), _const_spec((N_HEADS, 1)), _const_spec(wglu.shape)],
        out_specs=(row(ATT_WIDTH), row(ATT_WIDTH), row(ATT_WIDTH), row(ATT_WIDTH), row(ATT_WIDTH),
                   col, col, row(CONV_CH)),
        out_shape=out_shape,
        scratch_shapes=[pltpu.VMEM((N_HEADS, V7X_LANES), F32)],
        compiler_params=_cparams("arbitrary"),
        name="in_projection",
    )(x, g_mix, wqkv, wf_t, b_f, wglu)


LOG2E = 1.4426950408889634
BIAS_PIECES = 3


def _fox_prompt_kernel(q_ref, k_ref, v_ref, cum_ref, o_ref, kaug_ref, vt_ref, m_ref, l_ref, acc_ref, *, tq):
    qi = pl.program_id(2)
    seq = k_ref.shape[0]
    lane_k = lax.broadcasted_iota(I32, (seq, V7X_LANES), 1)

    @pl.when(qi == 0)
    def _():
        vt_ref[...] = v_ref[...].astype(F32).T.astype(BF16)
        k = k_ref[...]
        for h in range(2):
            c = cum_ref[h:h + 1, :] * LOG2E
            pieces = []
            for _ in range(BIAS_PIECES):
                piece = c.astype(BF16).astype(F32)
                pieces.append(piece)
                c = c - piece
            rows = jnp.concatenate(pieces + [jnp.zeros((V7X_SUBLANES - BIAS_PIECES, seq), F32)], axis=0)
            spare = (1 - h) * HEAD_DIM
            place = jnp.where((lax.broadcasted_iota(I32, (V7X_SUBLANES, V7X_LANES), 1) - spare)
                              == lax.broadcasted_iota(I32, (V7X_SUBLANES, V7X_LANES), 0), 1.0, 0.0)
            place = jnp.where(lax.broadcasted_iota(I32, place.shape, 0) < BIAS_PIECES, place, 0.0)
            extra = lax.dot_general(rows, place, (((0,), (0,)), ((), ())), preferred_element_type=F32)
            own = (lane_k >= h * HEAD_DIM) & (lane_k < (h + 1) * HEAD_DIM)
            kaug_ref[h] = jnp.where(own, k, extra.astype(BF16))

    q = q_ref[...]
    lane_q = lax.broadcasted_iota(I32, q.shape, 1)
    q_aug = []
    for h in range(2):
        own = (lane_q >= h * HEAD_DIM) & (lane_q < (h + 1) * HEAD_DIM)
        spare = (1 - h) * HEAD_DIM
        minus_one = (lane_q >= spare) & (lane_q < spare + BIAS_PIECES)
        q_aug.append(jnp.where(own, q, jnp.where(minus_one, -1.0, 0.0).astype(q.dtype)))
    m_ref[...] = jnp.full(m_ref.shape, NEG_INF, F32)
    l_ref[...] = jnp.zeros(l_ref.shape, F32)
    acc_ref[...] = jnp.zeros(acc_ref.shape, F32)

    def tile(kj, diagonal):
        ks = pl.multiple_of(kj * tq, tq)
        vt = vt_ref[:, pl.ds(ks, tq)]
        for h in range(2):
            st = lax.dot_general(kaug_ref[h, pl.ds(ks, tq), :], q_aug[h], NT_DIMS,
                                 preferred_element_type=F32)
            if diagonal:
                key = lax.broadcasted_iota(I32, st.shape, 0)
                qry = lax.broadcasted_iota(I32, st.shape, 1)
                st = jnp.where(key <= qry, st, NEG_INF)
            m_old = m_ref[h]
            m_new = jnp.maximum(m_old, jnp.max(st, axis=0, keepdims=True))
            alpha = jnp.exp2(m_old - m_new)
            p = jnp.exp2(st - m_new)
            l_ref[h] = alpha * l_ref[h] + jnp.sum(p, axis=0, keepdims=True)
            acc_ref[h] = alpha * acc_ref[h] + jnp.dot(vt, p.astype(BF16), preferred_element_type=F32)
            m_ref[h] = m_new

    def body(kj, carry):
        tile(kj, False)
        return carry

    lax.fori_loop(0, qi, body, 0)
    tile(qi, True)
    o0 = acc_ref[0] / l_ref[0]
    o1 = acc_ref[1] / l_ref[1]
    first_head = lax.broadcasted_iota(I32, o0.shape, 0) < HEAD_DIM
    o_ref[...] = jnp.where(first_head, o0, o1).T.astype(o_ref.dtype)


def _fox_prompt(q, kb, vb, cum_pairs, *, batch, seq, tq):
    nq = seq // tq
    pairs = N_HEADS // 2
    return pl.pallas_call(
        functools.partial(_fox_prompt_kernel, tq=tq),
        grid=(batch, pairs, nq),
        in_specs=[
            pl.BlockSpec((tq, V7X_LANES), lambda b, p, i: (b * nq + i, p)),
            pl.BlockSpec((seq, V7X_LANES), lambda b, p, i: (b, p)),
            pl.BlockSpec((seq, V7X_LANES), lambda b, p, i: (b, p)),
            pl.BlockSpec((None, 2, seq), lambda b, p, i: (p, 0, b)),
        ],
        out_specs=pl.BlockSpec((tq, V7X_LANES), lambda b, p, i: (b * nq + i, p)),
        out_shape=jax.ShapeDtypeStruct(q.shape, BF16),
        scratch_shapes=[pltpu.VMEM((2, seq, V7X_LANES), BF16), pltpu.VMEM((V7X_LANES, seq), BF16),
                        pltpu.VMEM((2, 1, tq), F32), pltpu.VMEM((2, 1, tq), F32),
                        pltpu.VMEM((2, V7X_LANES, tq), F32)],
        compiler_params=_cparams("arbitrary", "arbitrary", "arbitrary"),
        name="fox_prompt",
    )(q, kb, vb, cum_pairs)


def _fox_sample_kernel(pt_ref, q_ref, kn_ref, vn_ref, cn_ref, *rest, n_steps, pp, n_new):
    del pt_ref
    k_refs, v_refs, lf_refs = rest[:pp], rest[pp:2 * pp], rest[2 * pp:3 * pp]
    o_ref, qbd_ref, m_ref, l_ref, acc_ref, suf_ref = rest[3 * pp:]
    j = pl.program_id(1)
    rows = n_new * N_HEADS
    page = suf_ref.shape[1]
    own_head = (lax.broadcasted_iota(I32, (N_HEADS, ATT_WIDTH), 1) // HEAD_DIM
                == lax.broadcasted_iota(I32, (N_HEADS, ATT_WIDTH), 0))

    def update(s, vt):
        m_old = m_ref[...]
        m_new = jnp.maximum(m_old, jnp.max(s, axis=1, keepdims=True))
        alpha = jnp.exp(m_old - m_new)
        p = jnp.exp(s - m_new)
        l_ref[...] = alpha * l_ref[...] + jnp.sum(p, axis=1, keepdims=True)
        acc_ref[...] = alpha * acc_ref[...] + lax.dot_general(p.astype(BF16), vt, NT_DIMS,
                                                              preferred_element_type=F32)
        m_ref[...] = m_new

    @pl.when(j == 0)
    def _():
        q = q_ref[...]
        qbd = jnp.concatenate(
            [jnp.where(own_head, jnp.broadcast_to(q[t:t + 1, :], own_head.shape), 0.0) for t in range(n_new)],
            axis=0)
        qbd_ref[...] = qbd.astype(BF16)
        m_ref[...] = jnp.full(m_ref.shape, NEG_INF, F32)
        l_ref[...] = jnp.zeros(l_ref.shape, F32)
        acc_ref[...] = jnp.zeros(acc_ref.shape, F32)
        suf_ref[...] = jnp.zeros(suf_ref.shape, F32)
        cn = jnp.concatenate([cn_ref[...]] * n_new, axis=0)
        tok = lax.broadcasted_iota(I32, (rows, page), 0) // N_HEADS
        col = lax.broadcasted_iota(I32, (rows, page), 1)
        s = jnp.dot(qbd_ref[...], kn_ref[...], preferred_element_type=F32)
        update(jnp.where(col <= tok, s - cn, NEG_INF), vn_ref[...])

    @pl.when(j > 0)
    def _():
        lane = lax.broadcasted_iota(I32, suf_ref.shape, 1)
        kts, vts, biases = [], [], []
        suffix = suf_ref[...]
        for i in range(pp):
            lf = lf_refs[i][...]
            inc = lf
            shift = 1
            while shift < page:
                inc = inc + jnp.where(lane < page - shift, pltpu.roll(inc, page - shift, 1), 0.0)
                shift *= 2
            biases.append(jnp.concatenate([(inc - lf) + suffix] * n_new, axis=0))
            suffix = suffix + jnp.broadcast_to(inc[:, 0:1], inc.shape)
            kts.append(k_refs[i][...].reshape(ATT_WIDTH, page).astype(BF16))
            vts.append(v_refs[i][...].reshape(ATT_WIDTH, page).astype(BF16))
        suf_ref[...] = suffix
        s = jnp.dot(qbd_ref[...], jnp.concatenate(kts, axis=1), preferred_element_type=F32)
        update(s + jnp.concatenate(biases, axis=1), jnp.concatenate(vts, axis=1))

    @pl.when(j == n_steps - 1)
    def _():
        o = acc_ref[...] / l_ref[...]
        out = [jnp.sum(jnp.where(own_head, o[t * N_HEADS:(t + 1) * N_HEADS, :], 0.0), axis=0, keepdims=True)
               for t in range(n_new)]
        o_ref[...] = jnp.concatenate(out, axis=0)


def _fox_sample(q, kn_t, vn_t, cn, cache_kt, cache_vt, cache_lft, page_table, *, pp):
    bd, n_new, _ = q.shape
    n_pages = page_table.shape[1]
    page = cache_kt.shape[3]
    rows = n_new * N_HEADS
    n_steps = n_pages // pp + 1
    per_seq = lambda b, j, pt: (b, 0, 0)

    def pool_idx(i, nd):
        def index(b, j, pt):
            return (pt[b, n_pages - 1 - ((jnp.maximum(j, 1) - 1) * pp + i)],) + (0,) * nd
        return index

    kv_spec = lambda i: pl.BlockSpec((None, N_HEADS, HEAD_DIM, page), pool_idx(i, 3))
    lf_spec = lambda i: pl.BlockSpec((None, N_HEADS, page), pool_idx(i, 2))
    grid_spec = pltpu.PrefetchScalarGridSpec(
        num_scalar_prefetch=1,
        grid=(bd, n_steps),
        in_specs=[pl.BlockSpec((None, n_new, ATT_WIDTH), per_seq),
                  pl.BlockSpec((None, ATT_WIDTH, page), per_seq),
                  pl.BlockSpec((None, ATT_WIDTH, page), per_seq),
                  pl.BlockSpec((None, N_HEADS, page), per_seq)]
                 + [kv_spec(i) for i in range(pp)] + [kv_spec(i) for i in range(pp)]
                 + [lf_spec(i) for i in range(pp)],
        out_specs=pl.BlockSpec((None, n_new, ATT_WIDTH), per_seq),
        scratch_shapes=[pltpu.VMEM((rows, ATT_WIDTH), BF16), pltpu.VMEM((rows, 1), F32),
                        pltpu.VMEM((rows, 1), F32), pltpu.VMEM((rows, ATT_WIDTH), F32),
                        pltpu.VMEM((N_HEADS, page), F32)],
    )
    return pl.pallas_call(
        functools.partial(_fox_sample_kernel, n_steps=n_steps, pp=pp, n_new=n_new),
        grid_spec=grid_spec,
        out_shape=jax.ShapeDtypeStruct((bd, n_new, ATT_WIDTH), F32),
        compiler_params=_cparams("arbitrary", "arbitrary"),
        name="fox_sample",
    )(page_table, q, kn_t, vn_t, cn, *([cache_kt] * pp), *([cache_vt] * pp), *([cache_lft] * pp))


HALO = 32


def _dwconv_prompt_kernel(glu_ref, halo_ref, w_ref, b_ref, y_ref, cat_ref, *, tm, tiles_per_seq):
    i = pl.program_id(0)
    starts_sequence = (i % tiles_per_seq) == 0
    cat_ref[0:HALO, :] = jnp.where(starts_sequence, 0.0, halo_ref[...])
    cat_ref[HALO:HALO + tm, :] = glu_ref[...]
    off = HALO - (CONV_WIDTH - 1)
    acc = jnp.broadcast_to(b_ref[...], (tm, CONV_CH))
    for w in range(CONV_WIDTH):
        acc = acc + cat_ref[pl.ds(off + w, tm), :] * w_ref[w:w + 1, :]
    y_ref[...] = acc


def _dwconv_prompt(glu, w_dw, b_dw, *, seq, tm):
    t = glu.shape[0]
    per_tile = tm // HALO
    return pl.pallas_call(
        functools.partial(_dwconv_prompt_kernel, tm=tm, tiles_per_seq=seq // tm),
        grid=(t // tm,),
        in_specs=[pl.BlockSpec((tm, CONV_CH), lambda i: (i, 0)),
                  pl.BlockSpec((HALO, CONV_CH), lambda i: (jnp.maximum(i * per_tile - 1, 0), 0)),
                  _const_spec(w_dw.shape), _const_spec(b_dw.shape)],
        out_specs=pl.BlockSpec((tm, CONV_CH), lambda i: (i, 0)),
        out_shape=jax.ShapeDtypeStruct(glu.shape, F32),
        scratch_shapes=[pltpu.VMEM((HALO + tm, CONV_CH), F32)],
        compiler_params=_cparams("arbitrary"),
        name="dwconv_prompt",
    )(glu, glu, w_dw, b_dw)


def _dwconv_sample_kernel(hist_ref, new_ref, w_ref, b_ref, y_ref, *, n_new):
    hist = CONV_WIDTH - 1
    for t in range(n_new):
        acc = jnp.broadcast_to(b_ref[...], y_ref.shape[1:])
        for w in range(CONV_WIDTH):
            pos = t + w
            row = hist_ref[pos] if pos < hist else new_ref[pos - hist]
            acc = acc + row * w_ref[w:w + 1, :]
        y_ref[t] = acc


def _dwconv_sample(hist_tm, new_tm, w_dw, b_dw):
    n_new = new_tm.shape[0]
    return pl.pallas_call(
        functools.partial(_dwconv_sample_kernel, n_new=n_new),
        out_shape=jax.ShapeDtypeStruct(new_tm.shape, F32),
        name="dwconv_sample",
    )(hist_tm, new_tm, w_dw, b_dw)


def _mix_kernel(x_ref, o_ref, y_ref, gmix_ref, wgate_ref, watt_ref, lng_ref, lnb_ref, wconv_ref,
                wout_ref, gffn_ref, wpq_ref, h_ref, xb_ref, qp_ref):
    x = x_ref[...]
    y = y_ref[...]
    mu = jnp.mean(y, axis=-1, keepdims=True)
    var = jnp.mean(jnp.square(y - mu), axis=-1, keepdims=True)
    z = (y - mu) * lax.rsqrt(var + LN_EPS) * lng_ref[...] + lnb_ref[...]
    z = z * jax.nn.sigmoid(z)
    conv = jnp.dot(z.astype(BF16), wconv_ref[...], preferred_element_type=F32)
    att = jnp.dot(o_ref[...].astype(BF16), watt_ref[...], preferred_element_type=F32)
    u = _rms(x, gmix_ref[...]).astype(BF16)
    gates = jnp.dot(u, wgate_ref[...], preferred_element_type=F32)
    merged = jax.nn.sigmoid(gates[:, :D_MODEL]) * att + jax.nn.sigmoid(gates[:, D_MODEL:]) * conv
    h = x + jnp.dot(merged.astype(BF16), wout_ref[...], preferred_element_type=F32)
    h_ref[...] = h
    xb = _rms(h, gffn_ref[...])
    xb_ref[...] = xb
    qp_ref[...] = jnp.dot(xb.astype(BF16), wpq_ref[...], preferred_element_type=F32)


def _mix(x, o, ydw, g_mix, wgate, watt, ln_g, ln_b, wconv, wout, g_ffn, wpq, *, tm):
    t = x.shape[0]
    row = lambda w: pl.BlockSpec((tm, w), lambda i: (i, 0))
    nq = wpq.shape[1]
    return pl.pallas_call(
        _mix_kernel,
        grid=(t // tm,),
        in_specs=[row(D_MODEL), row(ATT_WIDTH), row(CONV_CH), _const_spec(g_mix.shape),
                  _const_spec(wgate.shape), _const_spec(watt.shape), _const_spec(ln_g.shape),
                  _const_spec(ln_b.shape), _const_spec(wconv.shape), _const_spec(wout.shape),
                  _const_spec(g_ffn.shape), _const_spec(wpq.shape)],
        out_specs=(row(D_MODEL), row(D_MODEL), row(nq)),
        out_shape=(jax.ShapeDtypeStruct((t, D_MODEL), F32), jax.ShapeDtypeStruct((t, D_MODEL), F32),
                   jax.ShapeDtypeStruct((t, nq), F32)),
        compiler_params=_cparams("arbitrary"),
        name="mix",
    )(x, o, ydw, g_mix, wgate, watt, ln_g, ln_b, wconv, wout, g_ffn, wpq)


CAND_ROWS = 56
ROUTE_GROUP = 2


def _candidate_tables(width):
    assert TOPK == 16
    k = TOPK
    pairs = ([(0, j) for j in range(16)] + [(1, j) for j in range(8)]
             + [(i, 0) if i >= 2 else None for i in range(16)] + [(i, 1) if i >= 2 else None for i in range(8)]
             + [(2, 2), (2, 3), (2, 4), (3, 2), (3, 3), (4, 2), None, None])
    assert len(pairs) == CAND_ROWS
    assert sorted(p for p in pairs if p) == sorted((i, j) for i in range(k) for j in range(k) if (i + 1) * (j + 1) <= k)
    pos = [[p[0] * k + p[1] if p else k * k + r] * width for r, p in enumerate(pairs)]
    pad = [[0.0 if p else NEG_INF] * width for p in pairs]
    return jnp.asarray(pos, I32), jnp.asarray(pad, F32)


def _pair_combine(a, b, op):
    t = a.shape[1]
    bc = lambda x, i, n: jnp.broadcast_to(x[i:i + 1, :], (n, t))
    r = lax.broadcasted_iota(I32, (V7X_SUBLANES, t), 0)
    ea = jnp.where(r < 3, bc(a, 2, 8), jnp.where(r < 5, bc(a, 3, 8), bc(a, 4, 8)))
    eb = jnp.where((r == 0) | (r == 3) | (r == 5), bc(b, 2, 8), jnp.where((r == 1) | (r == 4), bc(b, 3, 8), bc(b, 4, 8)))
    return jnp.concatenate([op(bc(a, 0, 16), b), op(bc(a, 1, 8), b[:8]), op(a, bc(b, 0, 16)),
                            op(a[:8], bc(b, 1, 8)), op(ea, eb)], axis=0)


def _topk_rows(x, k, order=None, payload=None):
    n, t = x.shape
    if order is None:
        order = lax.broadcasted_iota(I32, (n, t), 0)
    slot = lax.broadcasted_iota(I32, (k, t), 0)
    big = jnp.iinfo(jnp.int32).max

    def step(i, carry):
        x, vals, picks = carry
        m = jnp.max(x, axis=0, keepdims=True)
        first = jnp.min(jnp.where(x == m, order, big), axis=0, keepdims=True)
        hit = order == first
        pick = first if payload is None else jnp.max(jnp.where(hit, payload, -1), axis=0, keepdims=True)
        vals = jnp.where(slot == i, m, vals)
        picks = jnp.where(slot == i, pick, picks)
        return jnp.where(hit, NEG_INF, x), vals, picks

    _, vals, picks = lax.fori_loop(0, k, step, (x, jnp.zeros((k, t), F32), jnp.zeros((k, t), I32)))
    return vals, picks


def _peer_route_kernel(qp_ref, keys_ref, pos_ref, pad_ref, e_ref, g_ref, sv_ref, si_ref):
    tm = qp_ref.shape[0]

    def head(h, carry):
        s = []
        for p in range(2):
            c0 = pl.multiple_of((h * 2 + p) * HALF_KEY, HALF_KEY)
            q_hp = qp_ref[:, pl.ds(c0, HALF_KEY)].astype(BF16)
            s.append(lax.dot_general(keys_ref[p], q_hp, NT_DIMS, preferred_element_type=F32))
        sv_ref[h], si_ref[h] = _topk_rows(jnp.concatenate(s, axis=1), TOPK)
        return carry

    lax.fori_loop(0, PEER_HEADS, head, 0)
    for h0 in range(0, PEER_HEADS, ROUTE_GROUP):
        heads = range(h0, h0 + ROUTE_GROUP)
        side = lambda ref, p: jnp.concatenate([ref[h][:, p * tm:(p + 1) * tm] for h in heads], axis=1)
        comb = _pair_combine(side(sv_ref, 0), side(sv_ref, 1), lambda x, y: x + y) + pad_ref[...]
        cidx = _pair_combine(side(si_ref, 0), side(si_ref, 1), lambda x, y: x * N_KEYS + y)
        tv, eidx = _topk_rows(comb, TOPK, order=pos_ref[...], payload=cidx)
        ex = jnp.exp(tv - jnp.max(tv, axis=0, keepdims=True))
        gate = ex / jnp.sum(ex, axis=0, keepdims=True)
        for n, h in enumerate(heads):
            e_ref[h * TOPK:(h + 1) * TOPK, :] = eidx[:, n * tm:(n + 1) * tm] * HALF_ROWS + HALF_ROWS
            g_ref[h * TOPK:(h + 1) * TOPK, :] = gate[:, n * tm:(n + 1) * tm]


def _peer_route(qp, keys):
    t = qp.shape[0]
    tm = V7X_LANES
    rows = PEER_HEADS * TOPK
    tables = _candidate_tables(ROUTE_GROUP * tm)
    out = pl.BlockSpec((rows, tm), lambda i: (0, i))
    return pl.pallas_call(
        _peer_route_kernel,
        grid=(t // tm,),
        in_specs=[pl.BlockSpec((tm, qp.shape[1]), lambda i: (i, 0)), _const_spec(keys.shape)]
                 + [_const_spec(a.shape) for a in tables],
        out_specs=(out, out),
        out_shape=(jax.ShapeDtypeStruct((rows, t), I32), jax.ShapeDtypeStruct((rows, t), F32)),
        scratch_shapes=[pltpu.VMEM((PEER_HEADS, TOPK, 2 * tm), F32), pltpu.VMEM((PEER_HEADS, TOPK, 2 * tm), I32)],
        compiler_params=_cparams("arbitrary"),
        name="peer_route",
    )(qp, keys, *tables)


PAIRS = PEER_HEADS * TOPK
HALF_ROWS = D_MODEL // 2 // V7X_LANES
HI_MASK = -65536


def _pack_expert_table(tab):
    bits = lax.bitcast_convert_type(tab.astype(BF16), jnp.uint16).astype(jnp.uint32)
    word = bits[:, :D_MODEL // 2] | (bits[:, D_MODEL // 2:] << 16)
    flat = lax.bitcast_convert_type(word, I32).reshape(tab.shape[0] * HALF_ROWS, V7X_LANES)
    return jnp.pad(flat, ((HALF_ROWS, V7X_SUBLANES - HALF_ROWS), (0, 0)))


def _expert_halves(tab_ref, off):
    w = tab_ref[pl.ds(off, V7X_SUBLANES), :]
    return pltpu.bitcast(w << 16, F32), pltpu.bitcast(w & HI_MASK, F32)


def _row_halves(x, upper):
    pad = jnp.zeros((V7X_SUBLANES - HALF_ROWS, V7X_LANES), F32)
    chunk = lambda c: x[:, c * V7X_LANES:(c + 1) * V7X_LANES]
    tile = lambda rows: jnp.concatenate([pad] + rows if upper else rows + [pad], axis=0)
    return (tile([chunk(r) for r in range(HALF_ROWS)]), tile([chunk(HALF_ROWS + r) for r in range(HALF_ROWS)]))


def _pair_row_sums(ps):
    sub = lax.broadcasted_iota(I32, ps[0].shape, 0)
    level = [ps[0] + ps[4], ps[2] + ps[6], ps[1] + ps[5], ps[3] + ps[7]]
    for h in (2, 1):
        first = (sub & (2 * h - 1)) < h
        level = [jnp.where(first, x, pltpu.roll(y, h, 0)) + jnp.where(first, pltpu.roll(x, V7X_SUBLANES - h, 0), y)
                 for x, y in zip(level[0::2], level[1::2])]
    return level[0]


def _peer_dot_kernel(off_ref, x_ref, g_ref, tab_ref, w_ref, r_ref, a_ref, *, tb):
    def token(t, carry):
        x_row = x_ref[pl.ds(t, 1), :]
        x_low, x_up = _row_halves(x_row, False), _row_halves(x_row, True)
        offs = off_ref.at[t]
        base = pl.multiple_of(t * PAIRS, PAIRS)
        for grp in range(PAIRS // V7X_SUBLANES):
            ps = []
            for j in range(V7X_SUBLANES):
                upper = j >= HALF_ROWS
                off = offs[j * TOPK + grp]
                lo, hi = _expert_halves(tab_ref, off - HALF_ROWS if upper else off)
                x_lo, x_hi = x_up if upper else x_low
                ps.append(lo * x_lo + hi * x_hi)
            r_ref[pl.ds(base + grp * V7X_SUBLANES, V7X_SUBLANES), :] = _pair_row_sums(ps)
        return carry

    lax.fori_loop(0, tb, token, 0)
    ones = jnp.ones((V7X_SUBLANES, V7X_LANES), BF16)
    r = r_ref[...]
    hi = r.astype(BF16)
    lo = (r - hi.astype(F32)).astype(BF16)
    sums = (lax.dot_general(ones, hi, NT_DIMS, preferred_element_type=F32)
            + lax.dot_general(ones, lo, NT_DIMS, preferred_element_type=F32))
    for t in range(tb):
        a_ref[t:t + 1, :] = sums[0:1, t * PAIRS:(t + 1) * PAIRS]
    a = a_ref[...]
    gelu = a * (lax.erf(a * (2.0 ** -0.5)) + 1.0) * 0.5
    w_ref[...] = g_ref[...] * gelu


def _peer_dot(off, x, g, tab, *, tb):
    t = off.shape[0]
    return pl.pallas_call(
        functools.partial(_peer_dot_kernel, tb=tb),
        grid=(t // tb,),
        in_specs=[pl.BlockSpec((tb, PAIRS), lambda i: (i, 0), memory_space=pltpu.SMEM),
                  pl.BlockSpec((tb, D_MODEL), lambda i: (i, 0)),
                  pl.BlockSpec((tb, PAIRS), lambda i: (i, 0)),
                  _const_spec(tab.shape)],
        out_specs=pl.BlockSpec((tb, PAIRS), lambda i: (i, 0)),
        out_shape=jax.ShapeDtypeStruct((t, PAIRS), F32),
        scratch_shapes=[pltpu.VMEM((tb * PAIRS, V7X_LANES), F32), pltpu.VMEM((tb, PAIRS), F32)],
        compiler_params=_cparams("arbitrary"),
        name="peer_dot",
    )(off, x, g, tab)


def _peer_sum_kernel(off_ref, w_ref, tab_ref, y_ref, wb_ref, *, tb):
    n_acc = 2

    def spread(t):
        return jnp.broadcast_to(w_ref[pl.ds(t, 1), :], (PAIRS, PAIRS)).T

    wb_ref[...] = spread(0)

    def token_group(g, carry):
        t0 = pl.multiple_of(g * V7X_SUBLANES, V7X_SUBLANES)
        y_rows = y_ref.at[pl.ds(t0, V7X_SUBLANES), :]
        for j in range(V7X_SUBLANES):
            t = t0 + j
            zero = jnp.zeros((V7X_SUBLANES, V7X_LANES), F32)
            acc_lo, acc_hi = [zero] * n_acc, [zero] * n_acc
            offs = [off_ref.at[t, pl.ds(c * TOPK, TOPK)] for c in range(PEER_HEADS)]
            nxt = spread(jnp.minimum(t + 1, tb - 1))
            for n in range(PAIRS):
                lo, hi = _expert_halves(tab_ref, offs[n % PEER_HEADS][n // PEER_HEADS])
                wk = wb_ref[n:n + 1, :]
                acc_lo[n % n_acc] = acc_lo[n % n_acc] + wk * lo
                acc_hi[n % n_acc] = acc_hi[n % n_acc] + wk * hi
            for half, acc in enumerate((acc_lo[0] + acc_lo[1], acc_hi[0] + acc_hi[1])):
                for r in range(HALF_ROWS):
                    c = (half * HALF_ROWS + r) * V7X_LANES
                    y_rows[j:j + 1, c:c + V7X_LANES] = acc[r:r + 1, :]
            wb_ref[...] = nxt
        return carry

    lax.fori_loop(0, tb // V7X_SUBLANES, token_group, 0)


def _peer_sum(off, w, tab, *, tb):
    t = off.shape[0]
    return pl.pallas_call(
        functools.partial(_peer_sum_kernel, tb=tb),
        grid=(t // tb,),
        in_specs=[pl.BlockSpec((tb, PAIRS), lambda i: (i, 0), memory_space=pltpu.SMEM),
                  pl.BlockSpec((tb, PAIRS), lambda i: (i, 0)), _const_spec(tab.shape)],
        out_specs=pl.BlockSpec((tb, D_MODEL), lambda i: (i, 0)),
        out_shape=jax.ShapeDtypeStruct((t, D_MODEL), F32),
        scratch_shapes=[pltpu.VMEM((PAIRS, PAIRS), F32)],
        compiler_params=_cparams("arbitrary"),
        name="peer_sum",
    )(off, w, tab)


def _ple_kernel(h_ref, y_ref, p_ref, gple_ref, wple_ref, wgate_ref, gfin_ref, o_ref):
    h = h_ref[...] + y_ref[...]
    gate = jax.nn.sigmoid(jnp.dot(_rms(h, gple_ref[...]).astype(BF16), wgate_ref[...], preferred_element_type=F32))
    h = h + jnp.dot(p_ref[...].astype(BF16), wple_ref[...], preferred_element_type=F32) * gate
    o_ref[...] = _rms(h, gfin_ref[...])


def _ple(h, y, p, g_ple, wple, wgate, g_final, *, tm):
    t = h.shape[0]
    row = lambda w: pl.BlockSpec((tm, w), lambda i: (i, 0))
    return pl.pallas_call(
        _ple_kernel,
        grid=(t // tm,),
        in_specs=[row(D_MODEL), row(D_MODEL), row(p.shape[1]), _const_spec(g_ple.shape),
                  _const_spec(wple.shape), _const_spec(wgate.shape), _const_spec(g_final.shape)],
        out_specs=row(D_MODEL),
        out_shape=jax.ShapeDtypeStruct((t, D_MODEL), F32),
        compiler_params=_cparams("arbitrary"),
        name="ple_out",
    )(h, y, p, g_ple, wple, wgate, g_final)


def _trunk(x, o, ydw, p, wts, *, tm, tb):
    h, xb, qp = _mix(x, o, ydw, wts["g_mix"], wts["wgate"], wts["watt"], wts["ln_g"], wts["ln_b"],
                     wts["wconv"], wts["wout"], wts["g_ffn"], wts["wpq"], tm=tm)
    e_t, g_t = _peer_route(qp, wts["keys"])
    off = e_t.T
    g_visit = g_t.reshape(PEER_HEADS, TOPK, -1).transpose(2, 1, 0).reshape(-1, PAIRS)
    w = _peer_dot(off, xb, g_visit, wts["tab_u"], tb=tb)
    y = _peer_sum(off, w, wts["tab_v"], tb=2 * tb)
    return _ple(h, y, p, wts["g_ple"], wts["wple"], wts["wplegate"], wts["g_final"], tm=tm)


def kernel(x_prompt, x_sample, cache_k, cache_v, cache_lf, state_conv, page_table, p_prompt, p_sample,
           g_mix, w_in, b_f, w_dw, b_dw, ln_g, ln_b, w_conv_proj, w_att_proj, w_out, g_ffn, w_pq,
           sub_keys, exp_u, exp_v, g_ple, w_ple, w_ple_gate, g_final):
    assert g_mix.shape[0] == 1, "single-layer trunk"
    b, s, _ = x_prompt.shape
    bd, sd, _ = x_sample.shape
    tp, ts = b * s, bd * sd
    page = cache_k.shape[2]

    w = w_in[0]
    c_f, c_glu, c_gate = 3 * ATT_WIDTH, 3 * ATT_WIDTH + N_HEADS, 3 * ATT_WIDTH + N_HEADS + 2 * CONV_CH
    wqkv = w[:, :c_f].astype(BF16)
    wf_t = w[:, c_f:c_glu].T.astype(BF16)
    wglu = w[:, c_glu:c_gate].astype(BF16)
    row = lambda a: a.reshape(1, -1)
    w_dw_pad = jnp.pad(w_dw[0], ((0, 1), (0, 0)))
    wts = dict(
        g_mix=row(g_mix[0]), wgate=w[:, c_gate:].astype(BF16), watt=w_att_proj[0].astype(BF16),
        ln_g=row(ln_g[0]), ln_b=row(ln_b[0]), wconv=w_conv_proj[0].astype(BF16), wout=w_out[0].astype(BF16),
        g_ffn=row(g_ffn[0]), wpq=w_pq[0].astype(BF16), keys=sub_keys[0].astype(BF16),
        tab_u=_pack_expert_table(exp_u[0]), tab_v=_pack_expert_table(exp_v[0]),
        g_ple=row(g_ple[0]), wple=w_ple[0].astype(BF16), wplegate=w_ple_gate[0].astype(BF16),
        g_final=row(g_final))
    b_f_col = b_f[0].reshape(N_HEADS, 1)
    b_dw_row = row(b_dw[0])

    xp = x_prompt.reshape(tp, D_MODEL)
    q, k, v, kb, vb, lf, cum, glu = _in_projection(xp, wts["g_mix"], wqkv, wf_t, b_f_col, wglu, seg=s, tm=512,
                                                   q_scale=LOG2E * HEAD_DIM ** -0.5)
    o = _fox_prompt(q, kb, vb, cum.reshape(N_HEADS // 2, 2, tp), batch=b, seq=s, tq=512)
    ydw = _dwconv_prompt(glu, w_dw_pad, b_dw_row, seq=s, tm=512)
    y_prompt = _trunk(xp, o, ydw, p_prompt[0].reshape(tp, -1), wts, tm=256, tb=64).reshape(b, s, D_MODEL)
    k_prompt = k.reshape(1, b, s, N_HEADS, HEAD_DIM)
    v_prompt = v.reshape(1, b, s, N_HEADS, HEAD_DIM)
    lf_prompt = lf.T.reshape(1, b, s, N_HEADS)
    conv_prompt = glu.reshape(b, s, CONV_CH)[None, :, s - (CONV_WIDTH - 1):, :]

    xs = x_sample.reshape(ts, D_MODEL)
    q, k, v, kb, vb, lf, cum, glu = _in_projection(xs, wts["g_mix"], wqkv, wf_t, b_f_col, wglu, seg=sd, tm=ts,
                                                   q_scale=HEAD_DIM ** -0.5)
    page_t = lambda a: jnp.swapaxes(jnp.pad(a.reshape(bd, sd, ATT_WIDTH), ((0, 0), (0, page - sd), (0, 0))), 1, 2)
    cn = jnp.pad(cum.reshape(N_HEADS, bd, sd).transpose(1, 0, 2), ((0, 0), (0, 0), (0, page - sd)))
    o = _fox_sample(q.astype(F32).reshape(bd, sd, ATT_WIDTH), page_t(kb), page_t(vb), cn,
                    jnp.transpose(cache_k[0], (0, 2, 3, 1)), jnp.transpose(cache_v[0], (0, 2, 3, 1)),
                    jnp.swapaxes(cache_lf[0], 1, 2), page_table, pp=8).reshape(ts, ATT_WIDTH)
    glu_s = glu.reshape(bd, sd, CONV_CH)
    ydw = _dwconv_sample(jnp.swapaxes(state_conv[0], 0, 1), jnp.swapaxes(glu_s, 0, 1), w_dw_pad, b_dw_row)
    ydw = jnp.swapaxes(ydw, 0, 1).reshape(ts, CONV_CH)
    y_sample = _trunk(xs, o, ydw, p_sample[0].reshape(ts, -1), wts, tm=ts, tb=64).reshape(bd, sd, D_MODEL)
    k_sample = k.reshape(1, bd, sd, N_HEADS, HEAD_DIM)
    v_sample = v.reshape(1, bd, sd, N_HEADS, HEAD_DIM)
    lf_sample = lf.T.reshape(1, bd, sd, N_HEADS)
    conv_sample = jnp.concatenate([state_conv[0][:, sd:], glu_s], axis=1)[None]

    return (y_prompt, y_sample, k_prompt, v_prompt, lf_prompt, conv_prompt,
            k_sample, v_sample, lf_sample, conv_sample)
```

```python
import functools

import jax
import jax.numpy as jnp
from jax import lax
from jax.experimental import pallas as pl
from jax.experimental.pallas import tpu as pltpu

F32 = jnp.float32
BF16 = jnp.bfloat16
I32 = jnp.int32

D_MODEL = 1024
N_HEADS = 8
HEAD_DIM = 64
ATT_WIDTH = N_HEADS * HEAD_DIM
CONV_CH = D_MODEL // 2
CONV_WIDTH = 31
PEER_HEADS = 8
N_KEYS = 128
HALF_KEY = 128
TOPK = 16
N_EXPERTS = N_KEYS * N_KEYS
RMS_EPS = 1e-6
LN_EPS = 1e-5

V7X_LANES = 128
V7X_SUBLANES = 8
V7X_VMEM_LIMIT_BYTES = 56 * 1024 * 1024

NEG_INF = float("-inf")
HIGHEST = lax.Precision.HIGHEST
NT_DIMS = (((1,), (1,)), ((), ()))


def _cparams(*sem):
    return pltpu.CompilerParams(dimension_semantics=sem, vmem_limit_bytes=V7X_VMEM_LIMIT_BYTES)


def _const_spec(shape):
    nd = len(shape)
    return pl.BlockSpec(shape, lambda *_: (0,) * nd, pipeline_mode=pl.Buffered(1))


def _rms(x, g):
    return x * lax.rsqrt(jnp.mean(x * x, axis=-1, keepdims=True) + RMS_EPS) * g


def _log_sigmoid(z):
    return jnp.minimum(z, 0.0) - jnp.log1p(jnp.exp(-jnp.abs(z)))


def _inproj_kernel(x_ref, g_ref, wqkv_ref, wf_ref, bf_ref, wglu_ref,
                   q_ref, k_ref, v_ref, kb_ref, vb_ref, lf_ref, cum_ref, glu_ref, carry_ref,
                   *, tm, seg, q_scale):
    i = pl.program_id(0)
    u = _rms(x_ref[...], g_ref[...]).astype(BF16)
    qkv = jnp.dot(u, wqkv_ref[...], preferred_element_type=F32)
    q_ref[...] = (qkv[:, :ATT_WIDTH] * q_scale).astype(BF16)
    k = qkv[:, ATT_WIDTH:2 * ATT_WIDTH]
    v = qkv[:, 2 * ATT_WIDTH:]
    k_ref[...] = k
    v_ref[...] = v
    kb_ref[...] = k.astype(BF16)
    vb_ref[...] = v.astype(BF16)
    cab = jnp.dot(u, wglu_ref[...], preferred_element_type=F32)
    glu_ref[...] = cab[:, :CONV_CH] * jax.nn.sigmoid(cab[:, CONV_CH:])
    fl = lax.dot_general(wf_ref[...], u, NT_DIMS, preferred_element_type=F32)
    lf = _log_sigmoid(fl + bf_ref[...])
    lf_ref[...] = lf
    src = lax.broadcasted_iota(I32, (tm, tm), 0)
    dst = lax.broadcasted_iota(I32, (tm, tm), 1)
    keep = src <= dst
    if seg < tm:
        keep = keep & ((src // seg) == (dst // seg))
    tri = jnp.where(keep, 1.0, 0.0).astype(F32)
    cum = jnp.dot(lf, tri, preferred_element_type=F32, precision=HIGHEST)
    if seg > tm:
        @pl.when(i % (seg // tm) == 0)
        def _():
            carry_ref[...] = jnp.zeros_like(carry_ref)
        cum = cum + carry_ref[:, 0:1]
        carry_ref[...] = jnp.broadcast_to(cum[:, tm - 1:tm], carry_ref.shape)
    cum_ref[...] = cum


def _in_projection(x, g_mix, wqkv, wf_t, b_f, wglu, *, seg, tm, q_scale):
    t = x.shape[0]
    row = lambda w: pl.BlockSpec((tm, w), lambda i: (i, 0))
    col = pl.BlockSpec((N_HEADS, tm), lambda i: (0, i))
    out_shape = (
        jax.ShapeDtypeStruct((t, ATT_WIDTH), BF16),
        jax.ShapeDtypeStruct((t, ATT_WIDTH), F32),
        jax.ShapeDtypeStruct((t, ATT_WIDTH), F32),
        jax.ShapeDtypeStruct((t, ATT_WIDTH), BF16),
        jax.ShapeDtypeStruct((t, ATT_WIDTH), BF16),
        jax.ShapeDtypeStruct((N_HEADS, t), F32),
        jax.ShapeDtypeStruct((N_HEADS, t), F32),
        jax.ShapeDtypeStruct((t, CONV_CH), F32),
    )
    return pl.pallas_call(
        functools.partial(_inproj_kernel, tm=tm, seg=seg, q_scale=q_scale),
        grid=(t // tm,),
        in_specs=[row(D_MODEL), _const_spec((1, D_MODEL)), _const_spec(wqkv.shape),
                  _const_spec(wf_t.shape), _const_spec((N_HEADS, 1)), _const_spec(wglu.shape)],
        out_specs=(row(ATT_WIDTH), row(ATT_WIDTH), row(ATT_WIDTH), row(ATT_WIDTH), row(ATT_WIDTH),
                   col, col, row(CONV_CH)),
        out_shape=out_shape,
        scratch_shapes=[pltpu.VMEM((N_HEADS, V7X_LANES), F32)],
        compiler_params=_cparams("arbitrary"),
        name="in_projection",
    )(x, g_mix, wqkv, wf_t, b_f, wglu)


LOG2E = 1.4426950408889634
BIAS_PIECES = 3


def _fox_prompt_kernel(q_ref, k_ref, v_ref, cum_ref, o_ref, kaug_ref, vt_ref, m_ref, l_ref, acc_ref, *, tq):
    qi = pl.program_id(2)
    seq = k_ref.shape[0]
    lane_k = lax.broadcasted_iota(I32, (seq, V7X_LANES), 1)

    @pl.when(qi == 0)
    def _():
        vt_ref[...] = v_ref[...].astype(F32).T.astype(BF16)
        k = k_ref[...]
        for h in range(2):
            c = cum_ref[h:h + 1, :] * LOG2E
            pieces = []
            for _ in range(BIAS_PIECES):
                piece = c.astype(BF16).astype(F32)
                pieces.append(piece)
                c = c - piece
            rows = jnp.concatenate(pieces + [jnp.zeros((V7X_SUBLANES - BIAS_PIECES, seq), F32)], axis=0)
            spare = (1 - h) * HEAD_DIM
            place = jnp.where((lax.broadcasted_iota(I32, (V7X_SUBLANES, V7X_LANES), 1) - spare)
                              == lax.broadcasted_iota(I32, (V7X_SUBLANES, V7X_LANES), 0), 1.0, 0.0)
            place = jnp.where(lax.broadcasted_iota(I32, place.shape, 0) < BIAS_PIECES, place, 0.0)
            extra = lax.dot_general(rows, place, (((0,), (0,)), ((), ())), preferred_element_type=F32)
            own = (lane_k >= h * HEAD_DIM) & (lane_k < (h + 1) * HEAD_DIM)
            kaug_ref[h] = jnp.where(own, k, extra.astype(BF16))

    q = q_ref[...]
    lane_q = lax.broadcasted_iota(I32, q.shape, 1)
    q_aug = []
    for h in range(2):
        own = (lane_q >= h * HEAD_DIM) & (lane_q < (h + 1) * HEAD_DIM)
        spare = (1 - h) * HEAD_DIM
        minus_one = (lane_q >= spare) & (lane_q < spare + BIAS_PIECES)
        q_aug.append(jnp.where(own, q, jnp.where(minus_one, -1.0, 0.0).astype(q.dtype)))
    m_ref[...] = jnp.full(m_ref.shape, NEG_INF, F32)
    l_ref[...] = jnp.zeros(l_ref.shape, F32)
    acc_ref[...] = jnp.zeros(acc_ref.shape, F32)

    def tile(kj, diagonal):
        ks = pl.multiple_of(kj * tq, tq)
        vt = vt_ref[:, pl.ds(ks, tq)]
        for h in range(2):
            st = lax.dot_general(kaug_ref[h, pl.ds(ks, tq), :], q_aug[h], NT_DIMS,
                                 preferred_element_type=F32)
            if diagonal:
                key = lax.broadcasted_iota(I32, st.shape, 0)
                qry = lax.broadcasted_iota(I32, st.shape, 1)
                st = jnp.where(key <= qry, st, NEG_INF)
            m_old = m_ref[h]
            m_new = jnp.maximum(m_old, jnp.max(st, axis=0, keepdims=True))
            alpha = jnp.exp2(m_old - m_new)
            p = jnp.exp2(st - m_new)
            l_ref[h] = alpha * l_ref[h] + jnp.sum(p, axis=0, keepdims=True)
            acc_ref[h] = alpha * acc_ref[h] + jnp.dot(vt, p.astype(BF16), preferred_element_type=F32)
            m_ref[h] = m_new

    def body(kj, carry):
        tile(kj, False)
        return carry

    lax.fori_loop(0, qi, body, 0)
    tile(qi, True)
    o0 = acc_ref[0] / l_ref[0]
    o1 = acc_ref[1] / l_ref[1]
    first_head = lax.broadcasted_iota(I32, o0.shape, 0) < HEAD_DIM
    o_ref[...] = jnp.where(first_head, o0, o1).T.astype(o_ref.dtype)


def _fox_prompt(q, kb, vb, cum_pairs, *, batch, seq, tq):
    nq = seq // tq
    pairs = N_HEADS // 2
    return pl.pallas_call(
        functools.partial(_fox_prompt_kernel, tq=tq),
        grid=(batch, pairs, nq),
        in_specs=[
            pl.BlockSpec((tq, V7X_LANES), lambda b, p, i: (b * nq + i, p)),
            pl.BlockSpec((seq, V7X_LANES), lambda b, p, i: (b, p)),
            pl.BlockSpec((seq, V7X_LANES), lambda b, p, i: (b, p)),
            pl.BlockSpec((None, 2, seq), lambda b, p, i: (p, 0, b)),
        ],
        out_specs=pl.BlockSpec((tq, V7X_LANES), lambda b, p, i: (b * nq + i, p)),
        out_shape=jax.ShapeDtypeStruct(q.shape, BF16),
        scratch_shapes=[pltpu.VMEM((2, seq, V7X_LANES), BF16), pltpu.VMEM((V7X_LANES, seq), BF16),
                        pltpu.VMEM((2, 1, tq), F32), pltpu.VMEM((2, 1, tq), F32),
                        pltpu.VMEM((2, V7X_LANES, tq), F32)],
        compiler_params=_cparams("arbitrary", "arbitrary", "arbitrary"),
        name="fox_prompt",
    )(q, kb, vb, cum_pairs)


def _fox_sample_kernel(pt_ref, q_ref, kn_ref, vn_ref, cn_ref, *rest, n_steps, pp, n_new):
    del pt_ref
    k_refs, v_refs, lf_refs = rest[:pp], rest[pp:2 * pp], rest[2 * pp:3 * pp]
    o_ref, qbd_ref, m_ref, l_ref, acc_ref, suf_ref = rest[3 * pp:]
    j = pl.program_id(1)
    rows = n_new * N_HEADS
    page = suf_ref.shape[1]
    own_head = (lax.broadcasted_iota(I32, (N_HEADS, ATT_WIDTH), 1) // HEAD_DIM
                == lax.broadcasted_iota(I32, (N_HEADS, ATT_WIDTH), 0))

    def update(s, vt):
        m_old = m_ref[...]
        m_new = jnp.maximum(m_old, jnp.max(s, axis=1, keepdims=True))
        alpha = jnp.exp(m_old - m_new)
        p = jnp.exp(s - m_new)
        l_ref[...] = alpha * l_ref[...] + jnp.sum(p, axis=1, keepdims=True)
        acc_ref[...] = alpha * acc_ref[...] + lax.dot_general(p.astype(BF16), vt, NT_DIMS,
                                                              preferred_element_type=F32)
        m_ref[...] = m_new

    @pl.when(j == 0)
    def _():
        q = q_ref[...]
        qbd = jnp.concatenate(
            [jnp.where(own_head, jnp.broadcast_to(q[t:t + 1, :], own_head.shape), 0.0) for t in range(n_new)],
            axis=0)
        qbd_ref[...] = qbd.astype(BF16)
        m_ref[...] = jnp.full(m_ref.shape, NEG_INF, F32)
        l_ref[...] = jnp.zeros(l_ref.shape, F32)
        acc_ref[...] = jnp.zeros(acc_ref.shape, F32)
        suf_ref[...] = jnp.zeros(suf_ref.shape, F32)
        cn = jnp.concatenate([cn_ref[...]] * n_new, axis=0)
        tok = lax.broadcasted_iota(I32, (rows, page), 0) // N_HEADS
        col = lax.broadcasted_iota(I32, (rows, page), 1)
        s = jnp.dot(qbd_ref[...], kn_ref[...], preferred_element_type=F32)
        update(jnp.where(col <= tok, s - cn, NEG_INF), vn_ref[...])

    @pl.when(j > 0)
    def _():
        lane = lax.broadcasted_iota(I32, suf_ref.shape, 1)
        kts, vts, biases = [], [], []
        suffix = suf_ref[...]
        for i in range(pp):
            lf = lf_refs[i][...]
            inc = lf
            shift = 1
            while shift < page:
                inc = inc + jnp.where(lane < page - shift, pltpu.roll(inc, page - shift, 1), 0.0)
                shift *= 2
            biases.append(jnp.concatenate([(inc - lf) + suffix] * n_new, axis=0))
            suffix = suffix + jnp.broadcast_to(inc[:, 0:1], inc.shape)
            kts.append(k_refs[i][...].reshape(ATT_WIDTH, page).astype(BF16))
            vts.append(v_refs[i][...].reshape(ATT_WIDTH, page).astype(BF16))
        suf_ref[...] = suffix
        s = jnp.dot(qbd_ref[...], jnp.concatenate(kts, axis=1), preferred_element_type=F32)
        update(s + jnp.concatenate(biases, axis=1), jnp.concatenate(vts, axis=1))

    @pl.when(j == n_steps - 1)
    def _():
        o = acc_ref[...] / l_ref[...]
        out = [jnp.sum(jnp.where(own_head, o[t * N_HEADS:(t + 1) * N_HEADS, :], 0.0), axis=0, keepdims=True)
               for t in range(n_new)]
        o_ref[...] = jnp.concatenate(out, axis=0)


def _fox_sample(q, kn_t, vn_t, cn, cache_kt, cache_vt, cache_lft, page_table, *, pp):
    bd, n_new, _ = q.shape
    n_pages = page_table.shape[1]
    page = cache_kt.shape[3]
    rows = n_new * N_HEADS
    n_steps = n_pages // pp + 1
    per_seq = lambda b, j, pt: (b, 0, 0)

    def pool_idx(i, nd):
        def index(b, j, pt):
            return (pt[b, n_pages - 1 - ((jnp.maximum(j, 1) - 1) * pp + i)],) + (0,) * nd
        return index

    kv_spec = lambda i: pl.BlockSpec((None, N_HEADS, HEAD_DIM, page), pool_idx(i, 3))
    lf_spec = lambda i: pl.BlockSpec((None, N_HEADS, page), pool_idx(i, 2))
    grid_spec = pltpu.PrefetchScalarGridSpec(
        num_scalar_prefetch=1,
        grid=(bd, n_steps),
        in_specs=[pl.BlockSpec((None, n_new, ATT_WIDTH), per_seq),
                  pl.BlockSpec((None, ATT_WIDTH, page), per_seq),
                  pl.BlockSpec((None, ATT_WIDTH, page), per_seq),
                  pl.BlockSpec((None, N_HEADS, page), per_seq)]
                 + [kv_spec(i) for i in range(pp)] + [kv_spec(i) for i in range(pp)]
                 + [lf_spec(i) for i in range(pp)],
        out_specs=pl.BlockSpec((None, n_new, ATT_WIDTH), per_seq),
        scratch_shapes=[pltpu.VMEM((rows, ATT_WIDTH), BF16), pltpu.VMEM((rows, 1), F32),
                        pltpu.VMEM((rows, 1), F32), pltpu.VMEM((rows, ATT_WIDTH), F32),
                        pltpu.VMEM((N_HEADS, page), F32)],
    )
    return pl.pallas_call(
        functools.partial(_fox_sample_kernel, n_steps=n_steps, pp=pp, n_new=n_new),
        grid_spec=grid_spec,
        out_shape=jax.ShapeDtypeStruct((bd, n_new, ATT_WIDTH), F32),
        compiler_params=_cparams("arbitrary", "arbitrary"),
        name="fox_sample",
    )(page_table, q, kn_t, vn_t, cn, *([cache_kt] * pp), *([cache_vt] * pp), *([cache_lft] * pp))


HALO = 32


def _dwconv_prompt_kernel(glu_ref, halo_ref, w_ref, b_ref, y_ref, cat_ref, *, tm, tiles_per_seq):
    i = pl.program_id(0)
    starts_sequence = (i % tiles_per_seq) == 0
    cat_ref[0:HALO, :] = jnp.where(starts_sequence, 0.0, halo_ref[...])
    cat_ref[HALO:HALO + tm, :] = glu_ref[...]
    off = HALO - (CONV_WIDTH - 1)
    acc = jnp.broadcast_to(b_ref[...], (tm, CONV_CH))
    for w in range(CONV_WIDTH):
        acc = acc + cat_ref[pl.ds(off + w, tm), :] * w_ref[w:w + 1, :]
    y_ref[...] = acc


def _dwconv_prompt(glu, w_dw, b_dw, *, seq, tm):
    t = glu.shape[0]
    per_tile = tm // HALO
    return pl.pallas_call(
        functools.partial(_dwconv_prompt_kernel, tm=tm, tiles_per_seq=seq // tm),
        grid=(t // tm,),
        in_specs=[pl.BlockSpec((tm, CONV_CH), lambda i: (i, 0)),
                  pl.BlockSpec((HALO, CONV_CH), lambda i: (jnp.maximum(i * per_tile - 1, 0), 0)),
                  _const_spec(w_dw.shape), _const_spec(b_dw.shape)],
        out_specs=pl.BlockSpec((tm, CONV_CH), lambda i: (i, 0)),
        out_shape=jax.ShapeDtypeStruct(glu.shape, F32),
        scratch_shapes=[pltpu.VMEM((HALO + tm, CONV_CH), F32)],
        compiler_params=_cparams("arbitrary"),
        name="dwconv_prompt",
    )(glu, glu, w_dw, b_dw)


def _dwconv_sample_kernel(hist_ref, new_ref, w_ref, b_ref, y_ref, *, n_new):
    hist = CONV_WIDTH - 1
    for t in range(n_new):
        acc = jnp.broadcast_to(b_ref[...], y_ref.shape[1:])
        for w in range(CONV_WIDTH):
            pos = t + w
            row = hist_ref[pos] if pos < hist else new_ref[pos - hist]
            acc = acc + row * w_ref[w:w + 1, :]
        y_ref[t] = acc


def _dwconv_sample(hist_tm, new_tm, w_dw, b_dw):
    n_new = new_tm.shape[0]
    return pl.pallas_call(
        functools.partial(_dwconv_sample_kernel, n_new=n_new),
        out_shape=jax.ShapeDtypeStruct(new_tm.shape, F32),
        name="dwconv_sample",
    )(hist_tm, new_tm, w_dw, b_dw)


def _mix_kernel(x_ref, o_ref, y_ref, gmix_ref, wgate_ref, watt_ref, lng_ref, lnb_ref, wconv_ref,
                wout_ref, gffn_ref, wpq_ref, h_ref, xb_ref, qp_ref):
    x = x_ref[...]
    y = y_ref[...]
    mu = jnp.mean(y, axis=-1, keepdims=True)
    var = jnp.mean(jnp.square(y - mu), axis=-1, keepdims=True)
    z = (y - mu) * lax.rsqrt(var + LN_EPS) * lng_ref[...] + lnb_ref[...]
    z = z * jax.nn.sigmoid(z)
    conv = jnp.dot(z.astype(BF16), wconv_ref[...], preferred_element_type=F32)
    att = jnp.dot(o_ref[...].astype(BF16), watt_ref[...], preferred_element_type=F32)
    u = _rms(x, gmix_ref[...]).astype(BF16)
    gates = jnp.dot(u, wgate_ref[...], preferred_element_type=F32)
    merged = jax.nn.sigmoid(gates[:, :D_MODEL]) * att + jax.nn.sigmoid(gates[:, D_MODEL:]) * conv
    h = x + jnp.dot(merged.astype(BF16), wout_ref[...], preferred_element_type=F32)
    h_ref[...] = h
    xb = _rms(h, gffn_ref[...])
    xb_ref[...] = xb
    qp_ref[...] = jnp.dot(xb.astype(BF16), wpq_ref[...], preferred_element_type=F32)


def _mix(x, o, ydw, g_mix, wgate, watt, ln_g, ln_b, wconv, wout, g_ffn, wpq, *, tm):
    t = x.shape[0]
    row = lambda w: pl.BlockSpec((tm, w), lambda i: (i, 0))
    nq = wpq.shape[1]
    return pl.pallas_call(
        _mix_kernel,
        grid=(t // tm,),
        in_specs=[row(D_MODEL), row(ATT_WIDTH), row(CONV_CH), _const_spec(g_mix.shape),
                  _const_spec(wgate.shape), _const_spec(watt.shape), _const_spec(ln_g.shape),
                  _const_spec(ln_b.shape), _const_spec(wconv.shape), _const_spec(wout.shape),
                  _const_spec(g_ffn.shape), _const_spec(wpq.shape)],
        out_specs=(row(D_MODEL), row(D_MODEL), row(nq)),
        out_shape=(jax.ShapeDtypeStruct((t, D_MODEL), F32), jax.ShapeDtypeStruct((t, D_MODEL), F32),
                   jax.ShapeDtypeStruct((t, nq), F32)),
        compiler_params=_cparams("arbitrary"),
        name="mix",
    )(x, o, ydw, g_mix, wgate, watt, ln_g, ln_b, wconv, wout, g_ffn, wpq)


CAND_ROWS = 56
ROUTE_GROUP = 2


def _candidate_tables(width):
    assert TOPK == 16
    k = TOPK
    pairs = ([(0, j) for j in range(16)] + [(1, j) for j in range(8)]
             + [(i, 0) if i >= 2 else None for i in range(16)] + [(i, 1) if i >= 2 else None for i in range(8)]
             + [(2, 2), (2, 3), (2, 4), (3, 2), (3, 3), (4, 2), None, None])
    assert len(pairs) == CAND_ROWS
    assert sorted(p for p in pairs if p) == sorted((i, j) for i in range(k) for j in range(k) if (i + 1) * (j + 1) <= k)
    pos = [[p[0] * k + p[1] if p else k * k + r] * width for r, p in enumerate(pairs)]
    pad = [[0.0 if p else NEG_INF] * width for p in pairs]
    return jnp.asarray(pos, I32), jnp.asarray(pad, F32)


def _pair_combine(a, b, op):
    t = a.shape[1]
    bc = lambda x, i, n: jnp.broadcast_to(x[i:i + 1, :], (n, t))
    r = lax.broadcasted_iota(I32, (V7X_SUBLANES, t), 0)
    ea = jnp.where(r < 3, bc(a, 2, 8), jnp.where(r < 5, bc(a, 3, 8), bc(a, 4, 8)))
    eb = jnp.where((r == 0) | (r == 3) | (r == 5), bc(b, 2, 8), jnp.where((r == 1) | (r == 4), bc(b, 3, 8), bc(b, 4, 8)))
    return jnp.concatenate([op(bc(a, 0, 16), b), op(bc(a, 1, 8), b[:8]), op(a, bc(b, 0, 16)),
                            op(a[:8], bc(b, 1, 8)), op(ea, eb)], axis=0)


def _topk_rows(x, k, order=None, payload=None):
    n, t = x.shape
    if order is None:
        order = lax.broadcasted_iota(I32, (n, t), 0)
    slot = lax.broadcasted_iota(I32, (k, t), 0)
    big = jnp.iinfo(jnp.int32).max

    def step(i, carry):
        x, vals, picks = carry
        m = jnp.max(x, axis=0, keepdims=True)
        first = jnp.min(jnp.where(x == m, order, big), axis=0, keepdims=True)
        hit = order == first
        pick = first if payload is None else jnp.max(jnp.where(hit, payload, -1), axis=0, keepdims=True)
        vals = jnp.where(slot == i, m, vals)
        picks = jnp.where(slot == i, pick, picks)
        return jnp.where(hit, NEG_INF, x), vals, picks

    _, vals, picks = lax.fori_loop(0, k, step, (x, jnp.zeros((k, t), F32), jnp.zeros((k, t), I32)), unroll=2)
    return vals, picks


def _peer_route_kernel(qp_ref, keys_ref, pos_ref, pad_ref, e_ref, g_ref, sv_ref, si_ref):
    tm = qp_ref.shape[0]

    def head(h, carry):
        s = []
        for p in range(2):
            c0 = pl.multiple_of((h * 2 + p) * HALF_KEY, HALF_KEY)
            q_hp = qp_ref[:, pl.ds(c0, HALF_KEY)].astype(BF16)
            s.append(lax.dot_general(keys_ref[p], q_hp, NT_DIMS, preferred_element_type=F32))
        sv_ref[h], si_ref[h] = _topk_rows(jnp.concatenate(s, axis=1), TOPK)
        return carry

    lax.fori_loop(0, PEER_HEADS, head, 0)
    for h0 in range(0, PEER_HEADS, ROUTE_GROUP):
        heads = range(h0, h0 + ROUTE_GROUP)
        side = lambda ref, p: jnp.concatenate([ref[h][:, p * tm:(p + 1) * tm] for h in heads], axis=1)
        comb = _pair_combine(side(sv_ref, 0), side(sv_ref, 1), lambda x, y: x + y) + pad_ref[...]
        cidx = _pair_combine(side(si_ref, 0), side(si_ref, 1), lambda x, y: x * N_KEYS + y)
        tv, eidx = _topk_rows(comb, TOPK, order=pos_ref[...], payload=cidx)
        ex = jnp.exp(tv - jnp.max(tv, axis=0, keepdims=True))
        gate = ex / jnp.sum(ex, axis=0, keepdims=True)
        for n, h in enumerate(heads):
            e_ref[h * TOPK:(h + 1) * TOPK, :] = eidx[:, n * tm:(n + 1) * tm] * HALF_ROWS + HALF_ROWS
            g_ref[h * TOPK:(h + 1) * TOPK, :] = gate[:, n * tm:(n + 1) * tm]


def _peer_route(qp, keys):
    t = qp.shape[0]
    tm = V7X_LANES
    rows = PEER_HEADS * TOPK
    tables = _candidate_tables(ROUTE_GROUP * tm)
    out = pl.BlockSpec((rows, tm), lambda i: (0, i))
    return pl.pallas_call(
        _peer_route_kernel,
        grid=(t // tm,),
        in_specs=[pl.BlockSpec((tm, qp.shape[1]), lambda i: (i, 0)), _const_spec(keys.shape)]
                 + [_const_spec(a.shape) for a in tables],
        out_specs=(out, out),
        out_shape=(jax.ShapeDtypeStruct((rows, t), I32), jax.ShapeDtypeStruct((rows, t), F32)),
        scratch_shapes=[pltpu.VMEM((PEER_HEADS, TOPK, 2 * tm), F32), pltpu.VMEM((PEER_HEADS, TOPK, 2 * tm), I32)],
        compiler_params=_cparams("arbitrary"),
        name="peer_route",
    )(qp, keys, *tables)


PAIRS = PEER_HEADS * TOPK
HALF_ROWS = D_MODEL // 2 // V7X_LANES
HI_MASK = -65536


def _pack_expert_table(tab):
    bits = lax.bitcast_convert_type(tab.astype(BF16), jnp.uint16).astype(jnp.uint32)
    word = bits[:, :D_MODEL // 2] | (bits[:, D_MODEL // 2:] << 16)
    flat = lax.bitcast_convert_type(word, I32).reshape(tab.shape[0] * HALF_ROWS, V7X_LANES)
    return jnp.pad(flat, ((HALF_ROWS, V7X_SUBLANES - HALF_ROWS), (0, 0)))


def _expert_halves(tab_ref, off):
    w = tab_ref[pl.ds(off, V7X_SUBLANES), :]
    return pltpu.bitcast(w << 16, F32), pltpu.bitcast(w & HI_MASK, F32)


def _row_halves(x, upper):
    pad = jnp.zeros((V7X_SUBLANES - HALF_ROWS, V7X_LANES), F32)
    chunk = lambda c: x[:, c * V7X_LANES:(c + 1) * V7X_LANES]
    tile = lambda rows: jnp.concatenate([pad] + rows if upper else rows + [pad], axis=0)
    return (tile([chunk(r) for r in range(HALF_ROWS)]), tile([chunk(HALF_ROWS + r) for r in range(HALF_ROWS)]))


def _pair_row_sums(ps):
    sub = lax.broadcasted_iota(I32, ps[0].shape, 0)
    level = [ps[0] + ps[4], ps[2] + ps[6], ps[1] + ps[5], ps[3] + ps[7]]
    for h in (2, 1):
        first = (sub & (2 * h - 1)) < h
        level = [jnp.where(first, x, pltpu.roll(y, h, 0)) + jnp.where(first, pltpu.roll(x, V7X_SUBLANES - h, 0), y)
                 for x, y in zip(level[0::2], level[1::2])]
    return level[0]


def _peer_dot_kernel(off_ref, x_ref, g_ref, tab_ref, w_ref, r_ref, a_ref, *, tb):
    def token(t, carry):
        x_row = x_ref[pl.ds(t, 1), :]
        x_low, x_up = _row_halves(x_row, False), _row_halves(x_row, True)
        offs = off_ref.at[t]
        base = pl.multiple_of(t * PAIRS, PAIRS)
        for grp in range(PAIRS // V7X_SUBLANES):
            ps = []
            for j in range(V7X_SUBLANES):
                upper = j >= HALF_ROWS
                off = offs[j * TOPK + grp]
                lo, hi = _expert_halves(tab_ref, off - HALF_ROWS if upper else off)
                x_lo, x_hi = x_up if upper else x_low
                ps.append(lo * x_lo + hi * x_hi)
            r_ref[pl.ds(base + grp * V7X_SUBLANES, V7X_SUBLANES), :] = _pair_row_sums(ps)
        return carry

    lax.fori_loop(0, tb, token, 0)
    ones = jnp.ones((V7X_SUBLANES, V7X_LANES), BF16)
    r = r_ref[...]
    hi = r.astype(BF16)
    lo = (r - hi.astype(F32)).astype(BF16)
    sums = (lax.dot_general(ones, hi, NT_DIMS, preferred_element_type=F32)
            + lax.dot_general(ones, lo, NT_DIMS, preferred_element_type=F32))
    for t in range(tb):
        a_ref[t:t + 1, :] = sums[0:1, t * PAIRS:(t + 1) * PAIRS]
    a = a_ref[...]
    gelu = a * (lax.erf(a * (2.0 ** -0.5)) + 1.0) * 0.5
    w_ref[...] = g_ref[...] * gelu


def _peer_dot(off, x, g, tab, *, tb):
    t = off.shape[0]
    return pl.pallas_call(
        functools.partial(_peer_dot_kernel, tb=tb),
        grid=(t // tb,),
        in_specs=[pl.BlockSpec((tb, PAIRS), lambda i: (i, 0), memory_space=pltpu.SMEM),
                  pl.BlockSpec((tb, D_MODEL), lambda i: (i, 0)),
                  pl.BlockSpec((tb, PAIRS), lambda i: (i, 0)),
                  _const_spec(tab.shape)],
        out_specs=pl.BlockSpec((tb, PAIRS), lambda i: (i, 0)),
        out_shape=jax.ShapeDtypeStruct((t, PAIRS), F32),
        scratch_shapes=[pltpu.VMEM((tb * PAIRS, V7X_LANES), F32), pltpu.VMEM((tb, PAIRS), F32)],
        compiler_params=_cparams("arbitrary"),
        name="peer_dot",
    )(off, x, g, tab)


def _peer_sum_kernel(off_ref, w_ref, tab_ref, y_ref, wb_ref, *, tb):
    n_acc = 2

    def spread(t):
        return jnp.broadcast_to(w_ref[pl.ds(t, 1), :], (PAIRS, PAIRS)).T

    wb_ref[...] = spread(0)

    def token_group(g, carry):
        t0 = pl.multiple_of(g * V7X_SUBLANES, V7X_SUBLANES)
        y_rows = y_ref.at[pl.ds(t0, V7X_SUBLANES), :]
        for j in range(V7X_SUBLANES):
            t = t0 + j
            zero = jnp.zeros((V7X_SUBLANES, V7X_LANES), F32)
            acc_lo, acc_hi = [zero] * n_acc, [zero] * n_acc
            offs = [off_ref.at[t, pl.ds(c * TOPK, TOPK)] for c in range(PEER_HEADS)]
            nxt = spread(jnp.minimum(t + 1, tb - 1))
            for n in range(PAIRS):
                lo, hi = _expert_halves(tab_ref, offs[n % PEER_HEADS][n // PEER_HEADS])
                wk = wb_ref[n:n + 1, :]
                acc_lo[n % n_acc] = acc_lo[n % n_acc] + wk * lo
                acc_hi[n % n_acc] = acc_hi[n % n_acc] + wk * hi
            for half, acc in enumerate((acc_lo[0] + acc_lo[1], acc_hi[0] + acc_hi[1])):
                for r in range(HALF_ROWS):
                    c = (half * HALF_ROWS + r) * V7X_LANES
                    y_rows[j:j + 1, c:c + V7X_LANES] = acc[r:r + 1, :]
            wb_ref[...] = nxt
        return carry

    lax.fori_loop(0, tb // V7X_SUBLANES, token_group, 0)


def _peer_sum(off, w, tab, *, tb):
    t = off.shape[0]
    return pl.pallas_call(
        functools.partial(_peer_sum_kernel, tb=tb),
        grid=(t // tb,),
        in_specs=[pl.BlockSpec((tb, PAIRS), lambda i: (i, 0), memory_space=pltpu.SMEM),
                  pl.BlockSpec((tb, PAIRS), lambda i: (i, 0)), _const_spec(tab.shape)],
        out_specs=pl.BlockSpec((tb, D_MODEL), lambda i: (i, 0)),
        out_shape=jax.ShapeDtypeStruct((t, D_MODEL), F32),
        scratch_shapes=[pltpu.VMEM((PAIRS, PAIRS), F32)],
        compiler_params=_cparams("arbitrary"),
        name="peer_sum",
    )(off, w, tab)


def _ple_kernel(h_ref, y_ref, p_ref, gple_ref, wple_ref, wgate_ref, gfin_ref, o_ref):
    h = h_ref[...] + y_ref[...]
    gate = jax.nn.sigmoid(jnp.dot(_rms(h, gple_ref[...]).astype(BF16), wgate_ref[...], preferred_element_type=F32))
    h = h + jnp.dot(p_ref[...].astype(BF16), wple_ref[...], preferred_element_type=F32) * gate
    o_ref[...] = _rms(h, gfin_ref[...])


def _ple(h, y, p, g_ple, wple, wgate, g_final, *, tm):
    t = h.shape[0]
    row = lambda w: pl.BlockSpec((tm, w), lambda i: (i, 0))
    return pl.pallas_call(
        _ple_kernel,
        grid=(t // tm,),
        in_specs=[row(D_MODEL), row(D_MODEL), row(p.shape[1]), _const_spec(g_ple.shape),
                  _const_spec(wple.shape), _const_spec(wgate.shape), _const_spec(g_final.shape)],
        out_specs=row(D_MODEL),
        out_shape=jax.ShapeDtypeStruct((t, D_MODEL), F32),
        compiler_params=_cparams("arbitrary"),
        name="ple_out",
    )(h, y, p, g_ple, wple, wgate, g_final)


def _trunk(x, o, ydw, p, wts, *, tm, tb):
    h, xb, qp = _mix(x, o, ydw, wts["g_mix"], wts["wgate"], wts["watt"], wts["ln_g"], wts["ln_b"],
                     wts["wconv"], wts["wout"], wts["g_ffn"], wts["wpq"], tm=tm)
    e_t, g_t = _peer_route(qp, wts["keys"])
    off = e_t.T
    g_visit = g_t.reshape(PEER_HEADS, TOPK, -1).transpose(2, 1, 0).reshape(-1, PAIRS)
    w = _peer_dot(off, xb, g_visit, wts["tab_u"], tb=tb)
    y = _peer_sum(off, w, wts["tab_v"], tb=2 * tb)
    return _ple(h, y, p, wts["g_ple"], wts["wple"], wts["wplegate"], wts["g_final"], tm=tm)


def kernel(x_prompt, x_sample, cache_k, cache_v, cache_lf, state_conv, page_table, p_prompt, p_sample,
           g_mix, w_in, b_f, w_dw, b_dw, ln_g, ln_b, w_conv_proj, w_att_proj, w_out, g_ffn, w_pq,
           sub_keys, exp_u, exp_v, g_ple, w_ple, w_ple_gate, g_final):
    assert g_mix.shape[0] == 1, "single-layer trunk"
    b, s, _ = x_prompt.shape
    bd, sd, _ = x_sample.shape
    tp, ts = b * s, bd * sd
    page = cache_k.shape[2]

    w = w_in[0]
    c_f, c_glu, c_gate = 3 * ATT_WIDTH, 3 * ATT_WIDTH + N_HEADS, 3 * ATT_WIDTH + N_HEADS + 2 * CONV_CH
    wqkv = w[:, :c_f].astype(BF16)
    wf_t = w[:, c_f:c_glu].T.astype(BF16)
    wglu = w[:, c_glu:c_gate].astype(BF16)
    row = lambda a: a.reshape(1, -1)
    w_dw_pad = jnp.pad(w_dw[0], ((0, 1), (0, 0)))
    wts = dict(
        g_mix=row(g_mix[0]), wgate=w[:, c_gate:].astype(BF16), watt=w_att_proj[0].astype(BF16),
        ln_g=row(ln_g[0]), ln_b=row(ln_b[0]), wconv=w_conv_proj[0].astype(BF16), wout=w_out[0].astype(BF16),
        g_ffn=row(g_ffn[0]), wpq=w_pq[0].astype(BF16), keys=sub_keys[0].astype(BF16),
        tab_u=_pack_expert_table(exp_u[0]), tab_v=_pack_expert_table(exp_v[0]),
        g_ple=row(g_ple[0]), wple=w_ple[0].astype(BF16), wplegate=w_ple_gate[0].astype(BF16),
        g_final=row(g_final))
    b_f_col = b_f[0].reshape(N_HEADS, 1)
    b_dw_row = row(b_dw[0])

    xp = x_prompt.reshape(tp, D_MODEL)
    q, k, v, kb, vb, lf, cum, glu = _in_projection(xp, wts["g_mix"], wqkv, wf_t, b_f_col, wglu, seg=s, tm=512,
                                                   q_scale=LOG2E * HEAD_DIM ** -0.5)
    o = _fox_prompt(q, kb, vb, cum.reshape(N_HEADS // 2, 2, tp), batch=b, seq=s, tq=512)
    ydw = _dwconv_prompt(glu, w_dw_pad, b_dw_row, seq=s, tm=512)
    y_prompt = _trunk(xp, o, ydw, p_prompt[0].reshape(tp, -1), wts, tm=256, tb=64).reshape(b, s, D_MODEL)
    k_prompt = k.reshape(1, b, s, N_HEADS, HEAD_DIM)
    v_prompt = v.reshape(1, b, s, N_HEADS, HEAD_DIM)
    lf_prompt = lf.T.reshape(1, b, s, N_HEADS)
    conv_prompt = glu.reshape(b, s, CONV_CH)[None, :, s - (CONV_WIDTH - 1):, :]

    xs = x_sample.reshape(ts, D_MODEL)
    q, k, v, kb, vb, lf, cum, glu = _in_projection(xs, wts["g_mix"], wqkv, wf_t, b_f_col, wglu, seg=sd, tm=ts,
                                                   q_scale=HEAD_DIM ** -0.5)
    page_t = lambda a: jnp.swapaxes(jnp.pad(a.reshape(bd, sd, ATT_WIDTH), ((0, 0), (0, page - sd), (0, 0))), 1, 2)
    cn = jnp.pad(cum.reshape(N_HEADS, bd, sd).transpose(1, 0, 2), ((0, 0), (0, 0), (0, page - sd)))
    o = _fox_sample(q.astype(F32).reshape(bd, sd, ATT_WIDTH), page_t(kb), page_t(vb), cn,
                    jnp.transpose(cache_k[0], (0, 2, 3, 1)), jnp.transpose(cache_v[0], (0, 2, 3, 1)),
                    jnp.swapaxes(cache_lf[0], 1, 2), page_table, pp=16).reshape(ts, ATT_WIDTH)
    glu_s = glu.reshape(bd, sd, CONV_CH)
    ydw = _dwconv_sample(jnp.swapaxes(state_conv[0], 0, 1), jnp.swapaxes(glu_s, 0, 1), w_dw_pad, b_dw_row)
    ydw = jnp.swapaxes(ydw, 0, 1).reshape(ts, CONV_CH)
    y_sample = _trunk(xs, o, ydw, p_sample[0].reshape(ts, -1), wts, tm=ts, tb=64).reshape(bd, sd, D_MODEL)
    k_sample = k.reshape(1, bd, sd, N_HEADS, HEAD_DIM)
    v_sample = v.reshape(1, bd, sd, N_HEADS, HEAD_DIM)
    lf_sample = lf.T.reshape(1, bd, sd, N_HEADS)
    conv_sample = jnp.concatenate([state_conv[0][:, sd:], glu_s], axis=1)[None]

    return (y_prompt, y_sample, k_prompt, v_prompt, lf_prompt, conv_prompt,
            k_sample, v_sample, lf_sample, conv_sample)
```

```python
import functools

import jax
import jax.numpy as jnp
from jax import lax
from jax.experimental import pallas as pl
from jax.experimental.pallas import tpu as pltpu

F32 = jnp.float32
BF16 = jnp.bfloat16
I32 = jnp.int32

D_MODEL = 1024
N_HEADS = 8
HEAD_DIM = 64
ATT_WIDTH = N_HEADS * HEAD_DIM
CONV_CH = D_MODEL // 2
CONV_WIDTH = 31
PEER_HEADS = 8
N_KEYS = 128
HALF_KEY = 128
TOPK = 16
N_EXPERTS = N_KEYS * N_KEYS
RMS_EPS = 1e-6
LN_EPS = 1e-5

V7X_LANES = 128
V7X_SUBLANES = 8
V7X_VMEM_LIMIT_BYTES = 56 * 1024 * 1024

NEG_INF = float("-inf")
HIGHEST = lax.Precision.HIGHEST
NT_DIMS = (((1,), (1,)), ((), ()))


def _cparams(*sem):
    return pltpu.CompilerParams(dimension_semantics=sem, vmem_limit_bytes=V7X_VMEM_LIMIT_BYTES)


def _const_spec(shape):
    nd = len(shape)
    return pl.BlockSpec(shape, lambda *_: (0,) * nd, pipeline_mode=pl.Buffered(1))


def _rms(x, g):
    return x * lax.rsqrt(jnp.mean(x * x, axis=-1, keepdims=True) + RMS_EPS) * g


def _log_sigmoid(z):
    return jnp.minimum(z, 0.0) - jnp.log1p(jnp.exp(-jnp.abs(z)))


def _inproj_kernel(x_ref, g_ref, wqkv_ref, wf_ref, bf_ref, wglu_ref,
                   q_ref, k_ref, v_ref, kb_ref, vb_ref, lf_ref, cum_ref, glu_ref, carry_ref,
                   *, tm, seg, q_scale):
    i = pl.program_id(0)
    u = _rms(x_ref[...], g_ref[...]).astype(BF16)
    qkv = jnp.dot(u, wqkv_ref[...], preferred_element_type=F32)
    q_ref[...] = (qkv[:, :ATT_WIDTH] * q_scale).astype(BF16)
    k = qkv[:, ATT_WIDTH:2 * ATT_WIDTH]
    v = qkv[:, 2 * ATT_WIDTH:]
    k_ref[...] = k
    v_ref[...] = v
    kb_ref[...] = k.astype(BF16)
    vb_ref[...] = v.astype(BF16)
    cab = jnp.dot(u, wglu_ref[...], preferred_element_type=F32)
    glu_ref[...] = cab[:, :CONV_CH] * jax.nn.sigmoid(cab[:, CONV_CH:])
    fl = lax.dot_general(wf_ref[...], u, NT_DIMS, preferred_element_type=F32)
    lf = _log_sigmoid(fl + bf_ref[...])
    lf_ref[...] = lf
    src = lax.broadcasted_iota(I32, (tm, tm), 0)
    dst = lax.broadcasted_iota(I32, (tm, tm), 1)
    keep = src <= dst
    if seg < tm:
        keep = keep & ((src // seg) == (dst // seg))
    tri = jnp.where(keep, 1.0, 0.0).astype(F32)
    cum = jnp.dot(lf, tri, preferred_element_type=F32, precision=HIGHEST)
    if seg > tm:
        @pl.when(i % (seg // tm) == 0)
        def _():
            carry_ref[...] = jnp.zeros_like(carry_ref)
        cum = cum + carry_ref[:, 0:1]
        carry_ref[...] = jnp.broadcast_to(cum[:, tm - 1:tm], carry_ref.shape)
    cum_ref[...] = cum


def _in_projection(x, g_mix, wqkv, wf_t, b_f, wglu, *, seg, tm, q_scale):
    t = x.shape[0]
    row = lambda w: pl.BlockSpec((tm, w), lambda i: (i, 0))
    col = pl.BlockSpec((N_HEADS, tm), lambda i: (0, i))
    out_shape = (
        jax.ShapeDtypeStruct((t, ATT_WIDTH), BF16),
        jax.ShapeDtypeStruct((t, ATT_WIDTH), F32),
        jax.ShapeDtypeStruct((t, ATT_WIDTH), F32),
        jax.ShapeDtypeStruct((t, ATT_WIDTH), BF16),
        jax.ShapeDtypeStruct((t, ATT_WIDTH), BF16),
        jax.ShapeDtypeStruct((N_HEADS, t), F32),
        jax.ShapeDtypeStruct((N_HEADS, t), F32),
        jax.ShapeDtypeStruct((t, CONV_CH), F32),
    )
    return pl.pallas_call(
        functools.partial(_inproj_kernel, tm=tm, seg=seg, q_scale=q_scale),
        grid=(t // tm,),
        in_specs=[row(D_MODEL), _const_spec((1, D_MODEL)), _const_spec(wqkv.shape),
                  _const_spec(wf_t.shape), _const_spec((N_HEADS, 1)), _const_spec(wglu.shape)],
        out_specs=(row(ATT_WIDTH), row(ATT_WIDTH), row(ATT_WIDTH), row(ATT_WIDTH), row(ATT_WIDTH),
                   col, col, row(CONV_CH)),
        out_shape=out_shape,
        scratch_shapes=[pltpu.VMEM((N_HEADS, V7X_LANES), F32)],
        compiler_params=_cparams("arbitrary"),
        name="in_projection",
    )(x, g_mix, wqkv, wf_t, b_f, wglu)


LOG2E = 1.4426950408889634
BIAS_PIECES = 3


def _fox_prompt_kernel(q_ref, k_ref, v_ref, cum_ref, o_ref, kaug_ref, vt_ref, m_ref, l_ref, acc_ref, *, tq):
    qi = pl.program_id(2)
    seq = k_ref.shape[0]
    lane_k = lax.broadcasted_iota(I32, (seq, V7X_LANES), 1)

    @pl.when(qi == 0)
    def _():
        vt_ref[...] = v_ref[...].astype(F32).T.astype(BF16)
        k = k_ref[...]
        for h in range(2):
            c = cum_ref[h:h + 1, :] * LOG2E
            pieces = []
            for _ in range(BIAS_PIECES):
                piece = c.astype(BF16).astype(F32)
                pieces.append(piece)
                c = c - piece
            rows = jnp.concatenate(pieces + [jnp.zeros((V7X_SUBLANES - BIAS_PIECES, seq), F32)], axis=0)
            spare = (1 - h) * HEAD_DIM
            place = jnp.where((lax.broadcasted_iota(I32, (V7X_SUBLANES, V7X_LANES), 1) - spare)
                              == lax.broadcasted_iota(I32, (V7X_SUBLANES, V7X_LANES), 0), 1.0, 0.0)
            place = jnp.where(lax.broadcasted_iota(I32, place.shape, 0) < BIAS_PIECES, place, 0.0)
            extra = lax.dot_general(rows, place, (((0,), (0,)), ((), ())), preferred_element_type=F32)
            own = (lane_k >= h * HEAD_DIM) & (lane_k < (h + 1) * HEAD_DIM)
            kaug_ref[h] = jnp.where(own, k, extra.astype(BF16))

    q = q_ref[...]
    lane_q = lax.broadcasted_iota(I32, q.shape, 1)
    q_aug = []
    for h in range(2):
        own = (lane_q >= h * HEAD_DIM) & (lane_q < (h + 1) * HEAD_DIM)
        spare = (1 - h) * HEAD_DIM
        minus_one = (lane_q >= spare) & (lane_q < spare + BIAS_PIECES)
        q_aug.append(jnp.where(own, q, jnp.where(minus_one, -1.0, 0.0).astype(q.dtype)))
    m_ref[...] = jnp.full(m_ref.shape, NEG_INF, F32)
    l_ref[...] = jnp.zeros(l_ref.shape, F32)
    acc_ref[...] = jnp.zeros(acc_ref.shape, F32)

    def tile(kj, diagonal):
        ks = pl.multiple_of(kj * tq, tq)
        vt = vt_ref[:, pl.ds(ks, tq)]
        for h in range(2):
            st = lax.dot_general(kaug_ref[h, pl.ds(ks, tq), :], q_aug[h], NT_DIMS,
                                 preferred_element_type=F32)
            if diagonal:
                key = lax.broadcasted_iota(I32, st.shape, 0)
                qry = lax.broadcasted_iota(I32, st.shape, 1)
                st = jnp.where(key <= qry, st, NEG_INF)
            m_old = m_ref[h]
            m_new = jnp.maximum(m_old, jnp.max(st, axis=0, keepdims=True))
            alpha = jnp.exp2(m_old - m_new)
            p = jnp.exp2(st - m_new)
            l_ref[h] = alpha * l_ref[h] + jnp.sum(p, axis=0, keepdims=True)
            acc_ref[h] = alpha * acc_ref[h] + jnp.dot(vt, p.astype(BF16), preferred_element_type=F32)
            m_ref[h] = m_new

    def body(kj, carry):
        tile(kj, False)
        return carry

    lax.fori_loop(0, qi, body, 0)
    tile(qi, True)
    o0 = acc_ref[0] / l_ref[0]
    o1 = acc_ref[1] / l_ref[1]
    first_head = lax.broadcasted_iota(I32, o0.shape, 0) < HEAD_DIM
    o_ref[...] = jnp.where(first_head, o0, o1).T.astype(o_ref.dtype)


def _fox_prompt(q, kb, vb, cum_pairs, *, batch, seq, tq):
    nq = seq // tq
    pairs = N_HEADS // 2
    return pl.pallas_call(
        functools.partial(_fox_prompt_kernel, tq=tq),
        grid=(batch, pairs, nq),
        in_specs=[
            pl.BlockSpec((tq, V7X_LANES), lambda b, p, i: (b * nq + i, p)),
            pl.BlockSpec((seq, V7X_LANES), lambda b, p, i: (b, p)),
            pl.BlockSpec((seq, V7X_LANES), lambda b, p, i: (b, p)),
            pl.BlockSpec((None, 2, seq), lambda b, p, i: (p, 0, b)),
        ],
        out_specs=pl.BlockSpec((tq, V7X_LANES), lambda b, p, i: (b * nq + i, p)),
        out_shape=jax.ShapeDtypeStruct(q.shape, BF16),
        scratch_shapes=[pltpu.VMEM((2, seq, V7X_LANES), BF16), pltpu.VMEM((V7X_LANES, seq), BF16),
                        pltpu.VMEM((2, 1, tq), F32), pltpu.VMEM((2, 1, tq), F32),
                        pltpu.VMEM((2, V7X_LANES, tq), F32)],
        compiler_params=_cparams("arbitrary", "arbitrary", "arbitrary"),
        name="fox_prompt",
    )(q, kb, vb, cum_pairs)


def _fox_sample_kernel(pt_ref, q_ref, kn_ref, vn_ref, cn_ref, *rest, n_steps, pp, n_new):
    del pt_ref
    k_refs, v_refs, lf_refs = rest[:pp], rest[pp:2 * pp], rest[2 * pp:3 * pp]
    o_ref, qbd_ref, m_ref, l_ref, acc_ref, suf_ref = rest[3 * pp:]
    j = pl.program_id(1)
    rows = n_new * N_HEADS
    page = suf_ref.shape[1]
    own_head = (lax.broadcasted_iota(I32, (N_HEADS, ATT_WIDTH), 1) // HEAD_DIM
                == lax.broadcasted_iota(I32, (N_HEADS, ATT_WIDTH), 0))

    def update(s, vt):
        m_old = m_ref[...]
        m_new = jnp.maximum(m_old, jnp.max(s, axis=1, keepdims=True))
        alpha = jnp.exp(m_old - m_new)
        p = jnp.exp(s - m_new)
        l_ref[...] = alpha * l_ref[...] + jnp.sum(p, axis=1, keepdims=True)
        acc_ref[...] = alpha * acc_ref[...] + lax.dot_general(p.astype(BF16), vt, NT_DIMS,
                                                              preferred_element_type=F32)
        m_ref[...] = m_new

    @pl.when(j == 0)
    def _():
        q = q_ref[...]
        qbd = jnp.concatenate(
            [jnp.where(own_head, jnp.broadcast_to(q[t:t + 1, :], own_head.shape), 0.0) for t in range(n_new)],
            axis=0)
        qbd_ref[...] = qbd.astype(BF16)
        m_ref[...] = jnp.full(m_ref.shape, NEG_INF, F32)
        l_ref[...] = jnp.zeros(l_ref.shape, F32)
        acc_ref[...] = jnp.zeros(acc_ref.shape, F32)
        suf_ref[...] = jnp.zeros(suf_ref.shape, F32)
        cn = jnp.concatenate([cn_ref[...]] * n_new, axis=0)
        tok = lax.broadcasted_iota(I32, (rows, page), 0) // N_HEADS
        col = lax.broadcasted_iota(I32, (rows, page), 1)
        s = jnp.dot(qbd_ref[...], kn_ref[...], preferred_element_type=F32)
        update(jnp.where(col <= tok, s - cn, NEG_INF), vn_ref[...])

    @pl.when(j > 0)
    def _():
        lane = lax.broadcasted_iota(I32, suf_ref.shape, 1)
        kts, vts, biases = [], [], []
        suffix = suf_ref[...]
        for i in range(pp):
            lf = lf_refs[i][...]
            inc = lf
            shift = 1
            while shift < page:
                inc = inc + jnp.where(lane < page - shift, pltpu.roll(inc, page - shift, 1), 0.0)
                shift *= 2
            biases.append(jnp.concatenate([(inc - lf) + suffix] * n_new, axis=0))
            suffix = suffix + jnp.broadcast_to(inc[:, 0:1], inc.shape)
            kts.append(k_refs[i][...].reshape(ATT_WIDTH, page).astype(BF16))
            vts.append(v_refs[i][...].reshape(ATT_WIDTH, page).astype(BF16))
        suf_ref[...] = suffix
        s = jnp.dot(qbd_ref[...], jnp.concatenate(kts, axis=1), preferred_element_type=F32)
        update(s + jnp.concatenate(biases, axis=1), jnp.concatenate(vts, axis=1))

    @pl.when(j == n_steps - 1)
    def _():
        o = acc_ref[...] / l_ref[...]
        out = [jnp.sum(jnp.where(own_head, o[t * N_HEADS:(t + 1) * N_HEADS, :], 0.0), axis=0, keepdims=True)
               for t in range(n_new)]
        o_ref[...] = jnp.concatenate(out, axis=0)


def _fox_sample(q, kn_t, vn_t, cn, cache_kt, cache_vt, cache_lft, page_table, *, pp):
    bd, n_new, _ = q.shape
    n_pages = page_table.shape[1]
    page = cache_kt.shape[3]
    rows = n_new * N_HEADS
    n_steps = n_pages // pp + 1
    per_seq = lambda b, j, pt: (b, 0, 0)

    def pool_idx(i, nd):
        def index(b, j, pt):
            return (pt[b, n_pages - 1 - ((jnp.maximum(j, 1) - 1) * pp + i)],) + (0,) * nd
        return index

    kv_spec = lambda i: pl.BlockSpec((None, N_HEADS, HEAD_DIM, page), pool_idx(i, 3))
    lf_spec = lambda i: pl.BlockSpec((None, N_HEADS, page), pool_idx(i, 2))
    grid_spec = pltpu.PrefetchScalarGridSpec(
        num_scalar_prefetch=1,
        grid=(bd, n_steps),
        in_specs=[pl.BlockSpec((None, n_new, ATT_WIDTH), per_seq),
                  pl.BlockSpec((None, ATT_WIDTH, page), per_seq),
                  pl.BlockSpec((None, ATT_WIDTH, page), per_seq),
                  pl.BlockSpec((None, N_HEADS, page), per_seq)]
                 + [kv_spec(i) for i in range(pp)] + [kv_spec(i) for i in range(pp)]
                 + [lf_spec(i) for i in range(pp)],
        out_specs=pl.BlockSpec((None, n_new, ATT_WIDTH), per_seq),
        scratch_shapes=[pltpu.VMEM((rows, ATT_WIDTH), BF16), pltpu.VMEM((rows, 1), F32),
                        pltpu.VMEM((rows, 1), F32), pltpu.VMEM((rows, ATT_WIDTH), F32),
                        pltpu.VMEM((N_HEADS, page), F32)],
    )
    return pl.pallas_call(
        functools.partial(_fox_sample_kernel, n_steps=n_steps, pp=pp, n_new=n_new),
        grid_spec=grid_spec,
        out_shape=jax.ShapeDtypeStruct((bd, n_new, ATT_WIDTH), F32),
        compiler_params=_cparams("arbitrary", "arbitrary"),
        name="fox_sample",
    )(page_table, q, kn_t, vn_t, cn, *([cache_kt] * pp), *([cache_vt] * pp), *([cache_lft] * pp))


HALO = 32


def _dwconv_prompt_kernel(glu_ref, halo_ref, w_ref, b_ref, y_ref, cat_ref, *, tm, tiles_per_seq):
    i = pl.program_id(0)
    starts_sequence = (i % tiles_per_seq) == 0
    cat_ref[0:HALO, :] = jnp.where(starts_sequence, 0.0, halo_ref[...])
    cat_ref[HALO:HALO + tm, :] = glu_ref[...]
    off = HALO - (CONV_WIDTH - 1)
    acc = jnp.broadcast_to(b_ref[...], (tm, CONV_CH))
    for w in range(CONV_WIDTH):
        acc = acc + cat_ref[pl.ds(off + w, tm), :] * w_ref[w:w + 1, :]
    y_ref[...] = acc


def _dwconv_prompt(glu, w_dw, b_dw, *, seq, tm):
    t = glu.shape[0]
    per_tile = tm // HALO
    return pl.pallas_call(
        functools.partial(_dwconv_prompt_kernel, tm=tm, tiles_per_seq=seq // tm),
        grid=(t // tm,),
        in_specs=[pl.BlockSpec((tm, CONV_CH), lambda i: (i, 0)),
                  pl.BlockSpec((HALO, CONV_CH), lambda i: (jnp.maximum(i * per_tile - 1, 0), 0)),
                  _const_spec(w_dw.shape), _const_spec(b_dw.shape)],
        out_specs=pl.BlockSpec((tm, CONV_CH), lambda i: (i, 0)),
        out_shape=jax.ShapeDtypeStruct(glu.shape, F32),
        scratch_shapes=[pltpu.VMEM((HALO + tm, CONV_CH), F32)],
        compiler_params=_cparams("arbitrary"),
        name="dwconv_prompt",
    )(glu, glu, w_dw, b_dw)


def _dwconv_sample_kernel(hist_ref, new_ref, w_ref, b_ref, y_ref, *, n_new):
    hist = CONV_WIDTH - 1
    for t in range(n_new):
        acc = jnp.broadcast_to(b_ref[...], y_ref.shape[1:])
        for w in range(CONV_WIDTH):
            pos = t + w
            row = hist_ref[pos] if pos < hist else new_ref[pos - hist]
            acc = acc + row * w_ref[w:w + 1, :]
        y_ref[t] = acc


def _dwconv_sample(hist_tm, new_tm, w_dw, b_dw):
    n_new = new_tm.shape[0]
    return pl.pallas_call(
        functools.partial(_dwconv_sample_kernel, n_new=n_new),
        out_shape=jax.ShapeDtypeStruct(new_tm.shape, F32),
        name="dwconv_sample",
    )(hist_tm, new_tm, w_dw, b_dw)


def _mix_kernel(x_ref, o_ref, y_ref, gmix_ref, wgate_ref, watt_ref, lng_ref, lnb_ref, wconv_ref,
                wout_ref, gffn_ref, wpq_ref, h_ref, xb_ref, qp_ref):
    x = x_ref[...]
    y = y_ref[...]
    mu = jnp.mean(y, axis=-1, keepdims=True)
    var = jnp.mean(jnp.square(y - mu), axis=-1, keepdims=True)
    z = (y - mu) * lax.rsqrt(var + LN_EPS) * lng_ref[...] + lnb_ref[...]
    z = z * jax.nn.sigmoid(z)
    conv = jnp.dot(z.astype(BF16), wconv_ref[...], preferred_element_type=F32)
    att = jnp.dot(o_ref[...].astype(BF16), watt_ref[...], preferred_element_type=F32)
    u = _rms(x, gmix_ref[...]).astype(BF16)
    gates = jnp.dot(u, wgate_ref[...], preferred_element_type=F32)
    merged = jax.nn.sigmoid(gates[:, :D_MODEL]) * att + jax.nn.sigmoid(gates[:, D_MODEL:]) * conv
    h = x + jnp.dot(merged.astype(BF16), wout_ref[...], preferred_element_type=F32)
    h_ref[...] = h
    xb = _rms(h, gffn_ref[...])
    xb_ref[...] = xb
    qp_ref[...] = jnp.dot(xb.astype(BF16), wpq_ref[...], preferred_element_type=F32)


def _mix(x, o, ydw, g_mix, wgate, watt, ln_g, ln_b, wconv, wout, g_ffn, wpq, *, tm):
    t = x.shape[0]
    row = lambda w: pl.BlockSpec((tm, w), lambda i: (i, 0))
    nq = wpq.shape[1]
    return pl.pallas_call(
        _mix_kernel,
        grid=(t // tm,),
        in_specs=[row(D_MODEL), row(ATT_WIDTH), row(CONV_CH), _const_spec(g_mix.shape),
                  _const_spec(wgate.shape), _const_spec(watt.shape), _const_spec(ln_g.shape),
                  _const_spec(ln_b.shape), _const_spec(wconv.shape), _const_spec(wout.shape),
                  _const_spec(g_ffn.shape), _const_spec(wpq.shape)],
        out_specs=(row(D_MODEL), row(D_MODEL), row(nq)),
        out_shape=(jax.ShapeDtypeStruct((t, D_MODEL), F32), jax.ShapeDtypeStruct((t, D_MODEL), F32),
                   jax.ShapeDtypeStruct((t, nq), F32)),
        compiler_params=_cparams("arbitrary"),
        name="mix",
    )(x, o, ydw, g_mix, wgate, watt, ln_g, ln_b, wconv, wout, g_ffn, wpq)


CAND_ROWS = 56
ROUTE_GROUP = 2


def _candidate_tables(width):
    assert TOPK == 16
    k = TOPK
    pairs = ([(0, j) for j in range(16)] + [(1, j) for j in range(8)]
             + [(i, 0) if i >= 2 else None for i in range(16)] + [(i, 1) if i >= 2 else None for i in range(8)]
             + [(2, 2), (2, 3), (2, 4), (3, 2), (3, 3), (4, 2), None, None])
    assert len(pairs) == CAND_ROWS
    assert sorted(p for p in pairs if p) == sorted((i, j) for i in range(k) for j in range(k) if (i + 1) * (j + 1) <= k)
    pos = [[p[0] * k + p[1] if p else k * k + r] * width for r, p in enumerate(pairs)]
    pad = [[0.0 if p else NEG_INF] * width for p in pairs]
    return jnp.asarray(pos, I32), jnp.asarray(pad, F32)


def _pair_combine(a, b, op):
    t = a.shape[1]
    bc = lambda x, i, n: jnp.broadcast_to(x[i:i + 1, :], (n, t))
    r = lax.broadcasted_iota(I32, (V7X_SUBLANES, t), 0)
    ea = jnp.where(r < 3, bc(a, 2, 8), jnp.where(r < 5, bc(a, 3, 8), bc(a, 4, 8)))
    eb = jnp.where((r == 0) | (r == 3) | (r == 5), bc(b, 2, 8), jnp.where((r == 1) | (r == 4), bc(b, 3, 8), bc(b, 4, 8)))
    return jnp.concatenate([op(bc(a, 0, 16), b), op(bc(a, 1, 8), b[:8]), op(a, bc(b, 0, 16)),
                            op(a[:8], bc(b, 1, 8)), op(ea, eb)], axis=0)


def _topk_rows(x, k, order=None, payload=None):
    n, t = x.shape
    if order is None:
        order = lax.broadcasted_iota(I32, (n, t), 0)
    slot = lax.broadcasted_iota(I32, (k, t), 0)
    big = jnp.iinfo(jnp.int32).max

    def step(i, carry):
        x, vals, picks = carry
        m = jnp.max(x, axis=0, keepdims=True)
        first = jnp.min(jnp.where(x == m, order, big), axis=0, keepdims=True)
        hit = order == first
        pick = first if payload is None else jnp.max(jnp.where(hit, payload, -1), axis=0, keepdims=True)
        vals = jnp.where(slot == i, m, vals)
        picks = jnp.where(slot == i, pick, picks)
        return jnp.where(hit, NEG_INF, x), vals, picks

    _, vals, picks = lax.fori_loop(0, k, step, (x, jnp.zeros((k, t), F32), jnp.zeros((k, t), I32)), unroll=4)
    return vals, picks


def _peer_route_kernel(qp_ref, keys_ref, pos_ref, pad_ref, e_ref, g_ref, sv_ref, si_ref):
    tm = qp_ref.shape[0]

    def head(h, carry):
        s = []
        for p in range(2):
            c0 = pl.multiple_of((h * 2 + p) * HALF_KEY, HALF_KEY)
            q_hp = qp_ref[:, pl.ds(c0, HALF_KEY)].astype(BF16)
            s.append(lax.dot_general(keys_ref[p], q_hp, NT_DIMS, preferred_element_type=F32))
        sv_ref[h], si_ref[h] = _topk_rows(jnp.concatenate(s, axis=1), TOPK)
        return carry

    lax.fori_loop(0, PEER_HEADS, head, 0)
    for h0 in range(0, PEER_HEADS, ROUTE_GROUP):
        heads = range(h0, h0 + ROUTE_GROUP)
        side = lambda ref, p: jnp.concatenate([ref[h][:, p * tm:(p + 1) * tm] for h in heads], axis=1)
        comb = _pair_combine(side(sv_ref, 0), side(sv_ref, 1), lambda x, y: x + y) + pad_ref[...]
        cidx = _pair_combine(side(si_ref, 0), side(si_ref, 1), lambda x, y: x * N_KEYS + y)
        tv, eidx = _topk_rows(comb, TOPK, order=pos_ref[...], payload=cidx)
        ex = jnp.exp(tv - jnp.max(tv, axis=0, keepdims=True))
        gate = ex / jnp.sum(ex, axis=0, keepdims=True)
        for n, h in enumerate(heads):
            e_ref[h * TOPK:(h + 1) * TOPK, :] = eidx[:, n * tm:(n + 1) * tm] * HALF_ROWS + HALF_ROWS
            g_ref[h * TOPK:(h + 1) * TOPK, :] = gate[:, n * tm:(n + 1) * tm]


def _peer_route(qp, keys):
    t = qp.shape[0]
    tm = V7X_LANES
    rows = PEER_HEADS * TOPK
    tables = _candidate_tables(ROUTE_GROUP * tm)
    out = pl.BlockSpec((rows, tm), lambda i: (0, i))
    return pl.pallas_call(
        _peer_route_kernel,
        grid=(t // tm,),
        in_specs=[pl.BlockSpec((tm, qp.shape[1]), lambda i: (i, 0)), _const_spec(keys.shape)]
                 + [_const_spec(a.shape) for a in tables],
        out_specs=(out, out),
        out_shape=(jax.ShapeDtypeStruct((rows, t), I32), jax.ShapeDtypeStruct((rows, t), F32)),
        scratch_shapes=[pltpu.VMEM((PEER_HEADS, TOPK, 2 * tm), F32), pltpu.VMEM((PEER_HEADS, TOPK, 2 * tm), I32)],
        compiler_params=_cparams("arbitrary"),
        name="peer_route",
    )(qp, keys, *tables)


PAIRS = PEER_HEADS * TOPK
HALF_ROWS = D_MODEL // 2 // V7X_LANES
HI_MASK = -65536


def _pack_expert_table(tab):
    bits = lax.bitcast_convert_type(tab.astype(BF16), jnp.uint16).astype(jnp.uint32)
    word = bits[:, :D_MODEL // 2] | (bits[:, D_MODEL // 2:] << 16)
    flat = lax.bitcast_convert_type(word, I32).reshape(tab.shape[0] * HALF_ROWS, V7X_LANES)
    return jnp.pad(flat, ((HALF_ROWS, V7X_SUBLANES - HALF_ROWS), (0, 0)))


def _expert_halves(tab_ref, off):
    w = tab_ref[pl.ds(off, V7X_SUBLANES), :]
    return pltpu.bitcast(w << 16, F32), pltpu.bitcast(w & HI_MASK, F32)


def _row_halves(x, upper):
    pad = jnp.zeros((V7X_SUBLANES - HALF_ROWS, V7X_LANES), F32)
    chunk = lambda c: x[:, c * V7X_LANES:(c + 1) * V7X_LANES]
    tile = lambda rows: jnp.concatenate([pad] + rows if upper else rows + [pad], axis=0)
    return (tile([chunk(r) for r in range(HALF_ROWS)]), tile([chunk(HALF_ROWS + r) for r in range(HALF_ROWS)]))


def _pair_row_sums(ps):
    sub = lax.broadcasted_iota(I32, ps[0].shape, 0)
    level = [ps[0] + ps[4], ps[2] + ps[6], ps[1] + ps[5], ps[3] + ps[7]]
    for h in (2, 1):
        first = (sub & (2 * h - 1)) < h
        level = [jnp.where(first, x, pltpu.roll(y, h, 0)) + jnp.where(first, pltpu.roll(x, V7X_SUBLANES - h, 0), y)
                 for x, y in zip(level[0::2], level[1::2])]
    return level[0]


def _peer_dot_kernel(off_ref, x_ref, g_ref, tab_ref, w_ref, r_ref, a_ref, *, tb):
    def token(t, carry):
        x_row = x_ref[pl.ds(t, 1), :]
        x_low, x_up = _row_halves(x_row, False), _row_halves(x_row, True)
        offs = off_ref.at[t]
        base = pl.multiple_of(t * PAIRS, PAIRS)
        for grp in range(PAIRS // V7X_SUBLANES):
            ps = []
            for j in range(V7X_SUBLANES):
                upper = j >= HALF_ROWS
                off = offs[j * TOPK + grp]
                lo, hi = _expert_halves(tab_ref, off - HALF_ROWS if upper else off)
                x_lo, x_hi = x_up if upper else x_low
                ps.append(lo * x_lo + hi * x_hi)
            r_ref[pl.ds(base + grp * V7X_SUBLANES, V7X_SUBLANES), :] = _pair_row_sums(ps)
        return carry

    lax.fori_loop(0, tb, token, 0)
    ones = jnp.ones((V7X_SUBLANES, V7X_LANES), BF16)
    r = r_ref[...]
    hi = r.astype(BF16)
    lo = (r - hi.astype(F32)).astype(BF16)
    sums = (lax.dot_general(ones, hi, NT_DIMS, preferred_element_type=F32)
            + lax.dot_general(ones, lo, NT_DIMS, preferred_element_type=F32))
    for t in range(tb):
        a_ref[t:t + 1, :] = sums[0:1, t * PAIRS:(t + 1) * PAIRS]
    a = a_ref[...]
    gelu = a * (lax.erf(a * (2.0 ** -0.5)) + 1.0) * 0.5
    w_ref[...] = g_ref[...] * gelu


def _peer_dot(off, x, g, tab, *, tb):
    t = off.shape[0]
    return pl.pallas_call(
        functools.partial(_peer_dot_kernel, tb=tb),
        grid=(t // tb,),
        in_specs=[pl.BlockSpec((tb, PAIRS), lambda i: (i, 0), memory_space=pltpu.SMEM),
                  pl.BlockSpec((tb, D_MODEL), lambda i: (i, 0)),
                  pl.BlockSpec((tb, PAIRS), lambda i: (i, 0)),
                  _const_spec(tab.shape)],
        out_specs=pl.BlockSpec((tb, PAIRS), lambda i: (i, 0)),
        out_shape=jax.ShapeDtypeStruct((t, PAIRS), F32),
        scratch_shapes=[pltpu.VMEM((tb * PAIRS, V7X_LANES), F32), pltpu.VMEM((tb, PAIRS), F32)],
        compiler_params=_cparams("arbitrary"),
        name="peer_dot",
    )(off, x, g, tab)


def _peer_sum_kernel(off_ref, w_ref, tab_ref, y_ref, wb_ref, *, tb):
    n_acc = 2

    def spread(t):
        return jnp.broadcast_to(w_ref[pl.ds(t, 1), :], (PAIRS, PAIRS)).T

    wb_ref[...] = spread(0)

    def token_group(g, carry):
        t0 = pl.multiple_of(g * V7X_SUBLANES, V7X_SUBLANES)
        y_rows = y_ref.at[pl.ds(t0, V7X_SUBLANES), :]
        for j in range(V7X_SUBLANES):
            t = t0 + j
            zero = jnp.zeros((V7X_SUBLANES, V7X_LANES), F32)
            acc_lo, acc_hi = [zero] * n_acc, [zero] * n_acc
            offs = [off_ref.at[t, pl.ds(c * TOPK, TOPK)] for c in range(PEER_HEADS)]
            nxt = spread(jnp.minimum(t + 1, tb - 1))
            for n in range(PAIRS):
                lo, hi = _expert_halves(tab_ref, offs[n % PEER_HEADS][n // PEER_HEADS])
                wk = wb_ref[n:n + 1, :]
                acc_lo[n % n_acc] = acc_lo[n % n_acc] + wk * lo
                acc_hi[n % n_acc] = acc_hi[n % n_acc] + wk * hi
            for half, acc in enumerate((acc_lo[0] + acc_lo[1], acc_hi[0] + acc_hi[1])):
                for r in range(HALF_ROWS):
                    c = (half * HALF_ROWS + r) * V7X_LANES
                    y_rows[j:j + 1, c:c + V7X_LANES] = acc[r:r + 1, :]
            wb_ref[...] = nxt
        return carry

    lax.fori_loop(0, tb // V7X_SUBLANES, token_group, 0)


def _peer_sum(off, w, tab, *, tb):
    t = off.shape[0]
    return pl.pallas_call(
        functools.partial(_peer_sum_kernel, tb=tb),
        grid=(t // tb,),
        in_specs=[pl.BlockSpec((tb, PAIRS), lambda i: (i, 0), memory_space=pltpu.SMEM),
                  pl.BlockSpec((tb, PAIRS), lambda i: (i, 0)), _const_spec(tab.shape)],
        out_specs=pl.BlockSpec((tb, D_MODEL), lambda i: (i, 0)),
        out_shape=jax.ShapeDtypeStruct((t, D_MODEL), F32),
        scratch_shapes=[pltpu.VMEM((PAIRS, PAIRS), F32)],
        compiler_params=_cparams("arbitrary"),
        name="peer_sum",
    )(off, w, tab)


def _ple_kernel(h_ref, y_ref, p_ref, gple_ref, wple_ref, wgate_ref, gfin_ref, o_ref):
    h = h_ref[...] + y_ref[...]
    gate = jax.nn.sigmoid(jnp.dot(_rms(h, gple_ref[...]).astype(BF16), wgate_ref[...], preferred_element_type=F32))
    h = h + jnp.dot(p_ref[...].astype(BF16), wple_ref[...], preferred_element_type=F32) * gate
    o_ref[...] = _rms(h, gfin_ref[...])


def _ple(h, y, p, g_ple, wple, wgate, g_final, *, tm):
    t = h.shape[0]
    row = lambda w: pl.BlockSpec((tm, w), lambda i: (i, 0))
    return pl.pallas_call(
        _ple_kernel,
        grid=(t // tm,),
        in_specs=[row(D_MODEL), row(D_MODEL), row(p.shape[1]), _const_spec(g_ple.shape),
                  _const_spec(wple.shape), _const_spec(wgate.shape), _const_spec(g_final.shape)],
        out_specs=row(D_MODEL),
        out_shape=jax.ShapeDtypeStruct((t, D_MODEL), F32),
        compiler_params=_cparams("arbitrary"),
        name="ple_out",
    )(h, y, p, g_ple, wple, wgate, g_final)


def _trunk(x, o, ydw, p, wts, *, tm, tb):
    h, xb, qp = _mix(x, o, ydw, wts["g_mix"], wts["wgate"], wts["watt"], wts["ln_g"], wts["ln_b"],
                     wts["wconv"], wts["wout"], wts["g_ffn"], wts["wpq"], tm=tm)
    e_t, g_t = _peer_route(qp, wts["keys"])
    off = e_t.T
    g_visit = g_t.reshape(PEER_HEADS, TOPK, -1).transpose(2, 1, 0).reshape(-1, PAIRS)
    w = _peer_dot(off, xb, g_visit, wts["tab_u"], tb=tb)
    y = _peer_sum(off, w, wts["tab_v"], tb=2 * tb)
    return _ple(h, y, p, wts["g_ple"], wts["wple"], wts["wplegate"], wts["g_final"], tm=tm)


def kernel(x_prompt, x_sample, cache_k, cache_v, cache_lf, state_conv, page_table, p_prompt, p_sample,
           g_mix, w_in, b_f, w_dw, b_dw, ln_g, ln_b, w_conv_proj, w_att_proj, w_out, g_ffn, w_pq,
           sub_keys, exp_u, exp_v, g_ple, w_ple, w_ple_gate, g_final):
    assert g_mix.shape[0] == 1, "single-layer trunk"
    b, s, _ = x_prompt.shape
    bd, sd, _ = x_sample.shape
    tp, ts = b * s, bd * sd
    page = cache_k.shape[2]

    w = w_in[0]
    c_f, c_glu, c_gate = 3 * ATT_WIDTH, 3 * ATT_WIDTH + N_HEADS, 3 * ATT_WIDTH + N_HEADS + 2 * CONV_CH
    wqkv = w[:, :c_f].astype(BF16)
    wf_t = w[:, c_f:c_glu].T.astype(BF16)
    wglu = w[:, c_glu:c_gate].astype(BF16)
    row = lambda a: a.reshape(1, -1)
    w_dw_pad = jnp.pad(w_dw[0], ((0, 1), (0, 0)))
    wts = dict(
        g_mix=row(g_mix[0]), wgate=w[:, c_gate:].astype(BF16), watt=w_att_proj[0].astype(BF16),
        ln_g=row(ln_g[0]), ln_b=row(ln_b[0]), wconv=w_conv_proj[0].astype(BF16), wout=w_out[0].astype(BF16),
        g_ffn=row(g_ffn[0]), wpq=w_pq[0].astype(BF16), keys=sub_keys[0].astype(BF16),
        tab_u=_pack_expert_table(exp_u[0]), tab_v=_pack_expert_table(exp_v[0]),
        g_ple=row(g_ple[0]), wple=w_ple[0].astype(BF16), wplegate=w_ple_gate[0].astype(BF16),
        g_final=row(g_final))
    b_f_col = b_f[0].reshape(N_HEADS, 1)
    b_dw_row = row(b_dw[0])

    xp = x_prompt.reshape(tp, D_MODEL)
    q, k, v, kb, vb, lf, cum, glu = _in_projection(xp, wts["g_mix"], wqkv, wf_t, b_f_col, wglu, seg=s, tm=512,
                                                   q_scale=LOG2E * HEAD_DIM ** -0.5)
    o = _fox_prompt(q, kb, vb, cum.reshape(N_HEADS // 2, 2, tp), batch=b, seq=s, tq=512)
    ydw = _dwconv_prompt(glu, w_dw_pad, b_dw_row, seq=s, tm=512)
    y_prompt = _trunk(xp, o, ydw, p_prompt[0].reshape(tp, -1), wts, tm=512, tb=64).reshape(b, s, D_MODEL)
    k_prompt = k.reshape(1, b, s, N_HEADS, HEAD_DIM)
    v_prompt = v.reshape(1, b, s, N_HEADS, HEAD_DIM)
    lf_prompt = lf.T.reshape(1, b, s, N_HEADS)
    conv_prompt = glu.reshape(b, s, CONV_CH)[None, :, s - (CONV_WIDTH - 1):, :]

    xs = x_sample.reshape(ts, D_MODEL)
    q, k, v, kb, vb, lf, cum, glu = _in_projection(xs, wts["g_mix"], wqkv, wf_t, b_f_col, wglu, seg=sd, tm=ts,
                                                   q_scale=HEAD_DIM ** -0.5)
    page_t = lambda a: jnp.swapaxes(jnp.pad(a.reshape(bd, sd, ATT_WIDTH), ((0, 0), (0, page - sd), (0, 0))), 1, 2)
    cn = jnp.pad(cum.reshape(N_HEADS, bd, sd).transpose(1, 0, 2), ((0, 0), (0, 0), (0, page - sd)))
    o = _fox_sample(q.astype(F32).reshape(bd, sd, ATT_WIDTH), page_t(kb), page_t(vb), cn,
                    jnp.transpose(cache_k[0], (0, 2, 3, 1)), jnp.transpose(cache_v[0], (0, 2, 3, 1)),
                    jnp.swapaxes(cache_lf[0], 1, 2), page_table, pp=16).reshape(ts, ATT_WIDTH)
    glu_s = glu.reshape(bd, sd, CONV_CH)
    ydw = _dwconv_sample(jnp.swapaxes(state_conv[0], 0, 1), jnp.swapaxes(glu_s, 0, 1), w_dw_pad, b_dw_row)
    ydw = jnp.swapaxes(ydw, 0, 1).reshape(ts, CONV_CH)
    y_sample = _trunk(xs, o, ydw, p_sample[0].reshape(ts, -1), wts, tm=ts, tb=64).reshape(bd, sd, D_MODEL)
    k_sample = k.reshape(1, bd, sd, N_HEADS, HEAD_DIM)
    v_sample = v.reshape(1, bd, sd, N_HEADS, HEAD_DIM)
    lf_sample = lf.T.reshape(1, bd, sd, N_HEADS)
    conv_sample = jnp.concatenate([state_conv[0][:, sd:], glu_s], axis=1)[None]

    return (y_prompt, y_sample, k_prompt, v_prompt, lf_prompt, conv_prompt,
            k_sample, v_sample, lf_sample, conv_sample)
```

```python
import functools

import jax
import jax.numpy as jnp
from jax import lax
from jax.experimental import pallas as pl
from jax.experimental.pallas import tpu as pltpu

F32 = jnp.float32
BF16 = jnp.bfloat16
I32 = jnp.int32

D_MODEL = 1024
N_HEADS = 8
HEAD_DIM = 64
ATT_WIDTH = N_HEADS * HEAD_DIM
CONV_CH = D_MODEL // 2
CONV_WIDTH = 31
PEER_HEADS = 8
N_KEYS = 128
HALF_KEY = 128
TOPK = 16
N_EXPERTS = N_KEYS * N_KEYS
RMS_EPS = 1e-6
LN_EPS = 1e-5

V7X_LANES = 128
V7X_SUBLANES = 8
V7X_VMEM_LIMIT_BYTES = 56 * 1024 * 1024

NEG_INF = float("-inf")
HIGHEST = lax.Precision.HIGHEST
NT_DIMS = (((1,), (1,)), ((), ()))


def _cparams(*sem):
    return pltpu.CompilerParams(dimension_semantics=sem, vmem_limit_bytes=V7X_VMEM_LIMIT_BYTES)


def _const_spec(shape):
    nd = len(shape)
    return pl.BlockSpec(shape, lambda *_: (0,) * nd, pipeline_mode=pl.Buffered(1))


def _rms(x, g):
    return x * lax.rsqrt(jnp.mean(x * x, axis=-1, keepdims=True) + RMS_EPS) * g


def _log_sigmoid(z):
    return jnp.minimum(z, 0.0) - jnp.log1p(jnp.exp(-jnp.abs(z)))


def _inproj_kernel(x_ref, g_ref, wqkv_ref, wf_ref, bf_ref, wglu_ref,
                   q_ref, k_ref, v_ref, kb_ref, vb_ref, lf_ref, cum_ref, glu_ref, carry_ref,
                   *, tm, seg, q_scale):
    i = pl.program_id(0)
    u = _rms(x_ref[...], g_ref[...]).astype(BF16)
    qkv = jnp.dot(u, wqkv_ref[...], preferred_element_type=F32)
    q_ref[...] = (qkv[:, :ATT_WIDTH] * q_scale).astype(BF16)
    k = qkv[:, ATT_WIDTH:2 * ATT_WIDTH]
    v = qkv[:, 2 * ATT_WIDTH:]
    k_ref[...] = k
    v_ref[...] = v
    kb_ref[...] = k.astype(BF16)
    vb_ref[...] = v.astype(BF16)
    cab = jnp.dot(u, wglu_ref[...], preferred_element_type=F32)
    glu_ref[...] = cab[:, :CONV_CH] * jax.nn.sigmoid(cab[:, CONV_CH:])
    fl = lax.dot_general(wf_ref[...], u, NT_DIMS, preferred_element_type=F32)
    lf = _log_sigmoid(fl + bf_ref[...])
    lf_ref[...] = lf
    src = lax.broadcasted_iota(I32, (tm, tm), 0)
    dst = lax.broadcasted_iota(I32, (tm, tm), 1)
    keep = src <= dst
    if seg < tm:
        keep = keep & ((src // seg) == (dst // seg))
    tri = jnp.where(keep, 1.0, 0.0).astype(F32)
    cum = jnp.dot(lf, tri, preferred_element_type=F32, precision=HIGHEST)
    if seg > tm:
        @pl.when(i % (seg // tm) == 0)
        def _():
            carry_ref[...] = jnp.zeros_like(carry_ref)
        cum = cum + carry_ref[:, 0:1]
        carry_ref[...] = jnp.broadcast_to(cum[:, tm - 1:tm], carry_ref.shape)
    cum_ref[...] = cum


def _in_projection(x, g_mix, wqkv, wf_t, b_f, wglu, *, seg, tm, q_scale):
    t = x.shape[0]
    row = lambda w: pl.BlockSpec((tm, w), lambda i: (i, 0))
    col = pl.BlockSpec((N_HEADS, tm), lambda i: (0, i))
    out_shape = (
        jax.ShapeDtypeStruct((t, ATT_WIDTH), BF16),
        jax.ShapeDtypeStruct((t, ATT_WIDTH), F32),
        jax.ShapeDtypeStruct((t, ATT_WIDTH), F32),
        jax.ShapeDtypeStruct((t, ATT_WIDTH), BF16),
        jax.ShapeDtypeStruct((t, ATT_WIDTH), BF16),
        jax.ShapeDtypeStruct((N_HEADS, t), F32),
        jax.ShapeDtypeStruct((N_HEADS, t), F32),
        jax.ShapeDtypeStruct((t, CONV_CH), F32),
    )
    return pl.pallas_call(
        functools.partial(_inproj_kernel, tm=tm, seg=seg, q_scale=q_scale),
        grid=(t // tm,),
        in_specs=[row(D_MODEL), _const_spec((1, D_MODEL)), _const_spec(wqkv.shape),
                  _const_spec(wf_t.shape), _const_spec((N_HEADS, 1)), _const_spec(wglu.shape)],
        out_specs=(row(ATT_WIDTH), row(ATT_WIDTH), row(ATT_WIDTH), row(ATT_WIDTH), row(ATT_WIDTH),
                   col, col, row(CONV_CH)),
        out_shape=out_shape,
        scratch_shapes=[pltpu.VMEM((N_HEADS, V7X_LANES), F32)],
        compiler_params=_cparams("arbitrary"),
        name="in_projection",
    )(x, g_mix, wqkv, wf_t, b_f, wglu)


LOG2E = 1.4426950408889634
BIAS_PIECES = 3


def _fox_prompt_kernel(q_ref, k_ref, v_ref, cum_ref, o_ref, kaug_ref, vt_ref, m_ref, l_ref, acc_ref, *, tq):
    qi = pl.program_id(2)
    seq = k_ref.shape[0]
    lane_k = lax.broadcasted_iota(I32, (seq, V7X_LANES), 1)

    @pl.when(qi == 0)
    def _():
        vt_ref[...] = v_ref[...].astype(F32).T.astype(BF16)
        k = k_ref[...]
        for h in range(2):
            c = cum_ref[h:h + 1, :] * LOG2E
            pieces = []
            for _ in range(BIAS_PIECES):
                piece = c.astype(BF16).astype(F32)
                pieces.append(piece)
                c = c - piece
            rows = jnp.concatenate(pieces + [jnp.zeros((V7X_SUBLANES - BIAS_PIECES, seq), F32)], axis=0)
            spare = (1 - h) * HEAD_DIM
            place = jnp.where((lax.broadcasted_iota(I32, (V7X_SUBLANES, V7X_LANES), 1) - spare)
                              == lax.broadcasted_iota(I32, (V7X_SUBLANES, V7X_LANES), 0), 1.0, 0.0)
            place = jnp.where(lax.broadcasted_iota(I32, place.shape, 0) < BIAS_PIECES, place, 0.0)
            extra = lax.dot_general(rows, place, (((0,), (0,)), ((), ())), preferred_element_type=F32)
            own = (lane_k >= h * HEAD_DIM) & (lane_k < (h + 1) * HEAD_DIM)
            kaug_ref[h] = jnp.where(own, k, extra.astype(BF16))

    q = q_ref[...]
    lane_q = lax.broadcasted_iota(I32, q.shape, 1)
    q_aug = []
    for h in range(2):
        own = (lane_q >= h * HEAD_DIM) & (lane_q < (h + 1) * HEAD_DIM)
        spare = (1 - h) * HEAD_DIM
        minus_one = (lane_q >= spare) & (lane_q < spare + BIAS_PIECES)
        q_aug.append(jnp.where(own, q, jnp.where(minus_one, -1.0, 0.0).astype(q.dtype)))
    m_ref[...] = jnp.full(m_ref.shape, NEG_INF, F32)
    l_ref[...] = jnp.zeros(l_ref.shape, F32)
    acc_ref[...] = jnp.zeros(acc_ref.shape, F32)

    def tile(kj, diagonal):
        ks = pl.multiple_of(kj * tq, tq)
        vt = vt_ref[:, pl.ds(ks, tq)]
        for h in range(2):
            st = lax.dot_general(kaug_ref[h, pl.ds(ks, tq), :], q_aug[h], NT_DIMS,
                                 preferred_element_type=F32)
            if diagonal:
                key = lax.broadcasted_iota(I32, st.shape, 0)
                qry = lax.broadcasted_iota(I32, st.shape, 1)
                st = jnp.where(key <= qry, st, NEG_INF)
            m_old = m_ref[h]
            m_new = jnp.maximum(m_old, jnp.max(st, axis=0, keepdims=True))
            alpha = jnp.exp2(m_old - m_new)
            p = jnp.exp2(st - m_new)
            l_ref[h] = alpha * l_ref[h] + jnp.sum(p, axis=0, keepdims=True)
            acc_ref[h] = alpha * acc_ref[h] + jnp.dot(vt, p.astype(BF16), preferred_element_type=F32)
            m_ref[h] = m_new

    def body(kj, carry):
        tile(kj, False)
        return carry

    lax.fori_loop(0, qi, body, 0)
    tile(qi, True)
    o0 = acc_ref[0] / l_ref[0]
    o1 = acc_ref[1] / l_ref[1]
    first_head = lax.broadcasted_iota(I32, o0.shape, 0) < HEAD_DIM
    o_ref[...] = jnp.where(first_head, o0, o1).T.astype(o_ref.dtype)


def _fox_prompt(q, kb, vb, cum_pairs, *, batch, seq, tq):
    nq = seq // tq
    pairs = N_HEADS // 2
    return pl.pallas_call(
        functools.partial(_fox_prompt_kernel, tq=tq),
        grid=(batch, pairs, nq),
        in_specs=[
            pl.BlockSpec((tq, V7X_LANES), lambda b, p, i: (b * nq + i, p)),
            pl.BlockSpec((seq, V7X_LANES), lambda b, p, i: (b, p)),
            pl.BlockSpec((seq, V7X_LANES), lambda b, p, i: (b, p)),
            pl.BlockSpec((None, 2, seq), lambda b, p, i: (p, 0, b)),
        ],
        out_specs=pl.BlockSpec((tq, V7X_LANES), lambda b, p, i: (b * nq + i, p)),
        out_shape=jax.ShapeDtypeStruct(q.shape, BF16),
        scratch_shapes=[pltpu.VMEM((2, seq, V7X_LANES), BF16), pltpu.VMEM((V7X_LANES, seq), BF16),
                        pltpu.VMEM((2, 1, tq), F32), pltpu.VMEM((2, 1, tq), F32),
                        pltpu.VMEM((2, V7X_LANES, tq), F32)],
        compiler_params=_cparams("arbitrary", "arbitrary", "arbitrary"),
        name="fox_prompt",
    )(q, kb, vb, cum_pairs)


def _fox_sample_kernel(pt_ref, q_ref, kn_ref, vn_ref, cn_ref, *rest, n_steps, pp, n_new):
    del pt_ref
    k_refs, v_refs, lf_refs = rest[:pp], rest[pp:2 * pp], rest[2 * pp:3 * pp]
    o_ref, qbd_ref, m_ref, l_ref, acc_ref, suf_ref = rest[3 * pp:]
    j = pl.program_id(1)
    rows = n_new * N_HEADS
    page = suf_ref.shape[1]
    own_head = (lax.broadcasted_iota(I32, (N_HEADS, ATT_WIDTH), 1) // HEAD_DIM
                == lax.broadcasted_iota(I32, (N_HEADS, ATT_WIDTH), 0))

    def update(s, vt):
        m_old = m_ref[...]
        m_new = jnp.maximum(m_old, jnp.max(s, axis=1, keepdims=True))
        alpha = jnp.exp(m_old - m_new)
        p = jnp.exp(s - m_new)
        l_ref[...] = alpha * l_ref[...] + jnp.sum(p, axis=1, keepdims=True)
        acc_ref[...] = alpha * acc_ref[...] + lax.dot_general(p.astype(BF16), vt, NT_DIMS,
                                                              preferred_element_type=F32)
        m_ref[...] = m_new

    @pl.when(j == 0)
    def _():
        q = q_ref[...]
        qbd = jnp.concatenate(
            [jnp.where(own_head, jnp.broadcast_to(q[t:t + 1, :], own_head.shape), 0.0) for t in range(n_new)],
            axis=0)
        qbd_ref[...] = qbd.astype(BF16)
        m_ref[...] = jnp.full(m_ref.shape, NEG_INF, F32)
        l_ref[...] = jnp.zeros(l_ref.shape, F32)
        acc_ref[...] = jnp.zeros(acc_ref.shape, F32)
        suf_ref[...] = jnp.zeros(suf_ref.shape, F32)
        cn = jnp.concatenate([cn_ref[...]] * n_new, axis=0)
        tok = lax.broadcasted_iota(I32, (rows, page), 0) // N_HEADS
        col = lax.broadcasted_iota(I32, (rows, page), 1)
        s = jnp.dot(qbd_ref[...], kn_ref[...], preferred_element_type=F32)
        update(jnp.where(col <= tok, s - cn, NEG_INF), vn_ref[...])

    @pl.when(j > 0)
    def _():
        lane = lax.broadcasted_iota(I32, suf_ref.shape, 1)
        kts, vts, biases = [], [], []
        suffix = suf_ref[...]
        for i in range(pp):
            lf = lf_refs[i][...]
            inc = lf
            shift = 1
            while shift < page:
                inc = inc + jnp.where(lane < page - shift, pltpu.roll(inc, page - shift, 1), 0.0)
                shift *= 2
            biases.append(jnp.concatenate([(inc - lf) + suffix] * n_new, axis=0))
            suffix = suffix + jnp.broadcast_to(inc[:, 0:1], inc.shape)
            kts.append(k_refs[i][...].reshape(ATT_WIDTH, page).astype(BF16))
            vts.append(v_refs[i][...].reshape(ATT_WIDTH, page).astype(BF16))
        suf_ref[...] = suffix
        s = jnp.dot(qbd_ref[...], jnp.concatenate(kts, axis=1), preferred_element_type=F32)
        update(s + jnp.concatenate(biases, axis=1), jnp.concatenate(vts, axis=1))

    @pl.when(j == n_steps - 1)
    def _():
        o = acc_ref[...] / l_ref[...]
        out = [jnp.sum(jnp.where(own_head, o[t * N_HEADS:(t + 1) * N_HEADS, :], 0.0), axis=0, keepdims=True)
               for t in range(n_new)]
        o_ref[...] = jnp.concatenate(out, axis=0)


def _fox_sample(q, kn_t, vn_t, cn, cache_kt, cache_vt, cache_lft, page_table, *, pp):
    bd, n_new, _ = q.shape
    n_pages = page_table.shape[1]
    page = cache_kt.shape[3]
    rows = n_new * N_HEADS
    n_steps = n_pages // pp + 1
    per_seq = lambda b, j, pt: (b, 0, 0)

    def pool_idx(i, nd):
        def index(b, j, pt):
            return (pt[b, n_pages - 1 - ((jnp.maximum(j, 1) - 1) * pp + i)],) + (0,) * nd
        return index

    kv_spec = lambda i: pl.BlockSpec((None, N_HEADS, HEAD_DIM, page), pool_idx(i, 3))
    lf_spec = lambda i: pl.BlockSpec((None, N_HEADS, page), pool_idx(i, 2))
    grid_spec = pltpu.PrefetchScalarGridSpec(
        num_scalar_prefetch=1,
        grid=(bd, n_steps),
        in_specs=[pl.BlockSpec((None, n_new, ATT_WIDTH), per_seq),
                  pl.BlockSpec((None, ATT_WIDTH, page), per_seq),
                  pl.BlockSpec((None, ATT_WIDTH, page), per_seq),
                  pl.BlockSpec((None, N_HEADS, page), per_seq)]
                 + [kv_spec(i) for i in range(pp)] + [kv_spec(i) for i in range(pp)]
                 + [lf_spec(i) for i in range(pp)],
        out_specs=pl.BlockSpec((None, n_new, ATT_WIDTH), per_seq),
        scratch_shapes=[pltpu.VMEM((rows, ATT_WIDTH), BF16), pltpu.VMEM((rows, 1), F32),
                        pltpu.VMEM((rows, 1), F32), pltpu.VMEM((rows, ATT_WIDTH), F32),
                        pltpu.VMEM((N_HEADS, page), F32)],
    )
    return pl.pallas_call(
        functools.partial(_fox_sample_kernel, n_steps=n_steps, pp=pp, n_new=n_new),
        grid_spec=grid_spec,
        out_shape=jax.ShapeDtypeStruct((bd, n_new, ATT_WIDTH), F32),
        compiler_params=_cparams("arbitrary", "arbitrary"),
        name="fox_sample",
    )(page_table, q, kn_t, vn_t, cn, *([cache_kt] * pp), *([cache_vt] * pp), *([cache_lft] * pp))


HALO = 32


def _dwconv_prompt_kernel(glu_ref, halo_ref, w_ref, b_ref, y_ref, cat_ref, *, tm, tiles_per_seq):
    i = pl.program_id(0)
    starts_sequence = (i % tiles_per_seq) == 0
    cat_ref[0:HALO, :] = jnp.where(starts_sequence, 0.0, halo_ref[...])
    cat_ref[HALO:HALO + tm, :] = glu_ref[...]
    off = HALO - (CONV_WIDTH - 1)
    acc = jnp.broadcast_to(b_ref[...], (tm, CONV_CH))
    for w in range(CONV_WIDTH):
        acc = acc + cat_ref[pl.ds(off + w, tm), :] * w_ref[w:w + 1, :]
    y_ref[...] = acc


def _dwconv_prompt(glu, w_dw, b_dw, *, seq, tm):
    t = glu.shape[0]
    per_tile = tm // HALO
    return pl.pallas_call(
        functools.partial(_dwconv_prompt_kernel, tm=tm, tiles_per_seq=seq // tm),
        grid=(t // tm,),
        in_specs=[pl.BlockSpec((tm, CONV_CH), lambda i: (i, 0)),
                  pl.BlockSpec((HALO, CONV_CH), lambda i: (jnp.maximum(i * per_tile - 1, 0), 0)),
                  _const_spec(w_dw.shape), _const_spec(b_dw.shape)],
        out_specs=pl.BlockSpec((tm, CONV_CH), lambda i: (i, 0)),
        out_shape=jax.ShapeDtypeStruct(glu.shape, F32),
        scratch_shapes=[pltpu.VMEM((HALO + tm, CONV_CH), F32)],
        compiler_params=_cparams("arbitrary"),
        name="dwconv_prompt",
    )(glu, glu, w_dw, b_dw)


def _dwconv_sample_kernel(hist_ref, new_ref, w_ref, b_ref, y_ref, *, n_new):
    hist = CONV_WIDTH - 1
    for t in range(n_new):
        acc = jnp.broadcast_to(b_ref[...], y_ref.shape[1:])
        for w in range(CONV_WIDTH):
            pos = t + w
            row = hist_ref[pos] if pos < hist else new_ref[pos - hist]
            acc = acc + row * w_ref[w:w + 1, :]
        y_ref[t] = acc


def _dwconv_sample(hist_tm, new_tm, w_dw, b_dw):
    n_new = new_tm.shape[0]
    return pl.pallas_call(
        functools.partial(_dwconv_sample_kernel, n_new=n_new),
        out_shape=jax.ShapeDtypeStruct(new_tm.shape, F32),
        name="dwconv_sample",
    )(hist_tm, new_tm, w_dw, b_dw)


def _mix_kernel(x_ref, o_ref, y_ref, gmix_ref, wgate_ref, watt_ref, lng_ref, lnb_ref, wconv_ref,
                wout_ref, gffn_ref, wpq_ref, h_ref, xb_ref, qp_ref):
    x = x_ref[...]
    y = y_ref[...]
    mu = jnp.mean(y, axis=-1, keepdims=True)
    var = jnp.mean(jnp.square(y - mu), axis=-1, keepdims=True)
    z = (y - mu) * lax.rsqrt(var + LN_EPS) * lng_ref[...] + lnb_ref[...]
    z = z * jax.nn.sigmoid(z)
    conv = jnp.dot(z.astype(BF16), wconv_ref[...], preferred_element_type=F32)
    att = jnp.dot(o_ref[...].astype(BF16), watt_ref[...], preferred_element_type=F32)
    u = _rms(x, gmix_ref[...]).astype(BF16)
    gates = jnp.dot(u, wgate_ref[...], preferred_element_type=F32)
    merged = jax.nn.sigmoid(gates[:, :D_MODEL]) * att + jax.nn.sigmoid(gates[:, D_MODEL:]) * conv
    h = x + jnp.dot(merged.astype(BF16), wout_ref[...], preferred_element_type=F32)
    h_ref[...] = h
    xb = _rms(h, gffn_ref[...])
    xb_ref[...] = xb
    qp_ref[...] = jnp.dot(xb.astype(BF16), wpq_ref[...], preferred_element_type=F32)


def _mix(x, o, ydw, g_mix, wgate, watt, ln_g, ln_b, wconv, wout, g_ffn, wpq, *, tm):
    t = x.shape[0]
    row = lambda w: pl.BlockSpec((tm, w), lambda i: (i, 0))
    nq = wpq.shape[1]
    return pl.pallas_call(
        _mix_kernel,
        grid=(t // tm,),
        in_specs=[row(D_MODEL), row(ATT_WIDTH), row(CONV_CH), _const_spec(g_mix.shape),
                  _const_spec(wgate.shape), _const_spec(watt.shape), _const_spec(ln_g.shape),
                  _const_spec(ln_b.shape), _const_spec(wconv.shape), _const_spec(wout.shape),
                  _const_spec(g_ffn.shape), _const_spec(wpq.shape)],
        out_specs=(row(D_MODEL), row(D_MODEL), row(nq)),
        out_shape=(jax.ShapeDtypeStruct((t, D_MODEL), F32), jax.ShapeDtypeStruct((t, D_MODEL), F32),
                   jax.ShapeDtypeStruct((t, nq), F32)),
        compiler_params=_cparams("arbitrary"),
        name="mix",
    )(x, o, ydw, g_mix, wgate, watt, ln_g, ln_b, wconv, wout, g_ffn, wpq)


CAND_ROWS = 56
ROUTE_GROUP = 4


def _candidate_tables(width):
    assert TOPK == 16
    k = TOPK
    pairs = ([(0, j) for j in range(16)] + [(1, j) for j in range(8)]
             + [(i, 0) if i >= 2 else None for i in range(16)] + [(i, 1) if i >= 2 else None for i in range(8)]
             + [(2, 2), (2, 3), (2, 4), (3, 2), (3, 3), (4, 2), None, None])
    assert len(pairs) == CAND_ROWS
    assert sorted(p for p in pairs if p) == sorted((i, j) for i in range(k) for j in range(k) if (i + 1) * (j + 1) <= k)
    pos = [[p[0] * k + p[1] if p else k * k + r] * width for r, p in enumerate(pairs)]
    pad = [[0.0 if p else NEG_INF] * width for p in pairs]
    return jnp.asarray(pos, I32), jnp.asarray(pad, F32)


def _pair_combine(a, b, op):
    t = a.shape[1]
    bc = lambda x, i, n: jnp.broadcast_to(x[i:i + 1, :], (n, t))
    r = lax.broadcasted_iota(I32, (V7X_SUBLANES, t), 0)
    ea = jnp.where(r < 3, bc(a, 2, 8), jnp.where(r < 5, bc(a, 3, 8), bc(a, 4, 8)))
    eb = jnp.where((r == 0) | (r == 3) | (r == 5), bc(b, 2, 8), jnp.where((r == 1) | (r == 4), bc(b, 3, 8), bc(b, 4, 8)))
    return jnp.concatenate([op(bc(a, 0, 16), b), op(bc(a, 1, 8), b[:8]), op(a, bc(b, 0, 16)),
                            op(a[:8], bc(b, 1, 8)), op(ea, eb)], axis=0)


def _topk_rows(x, k, order=None, payload=None):
    n, t = x.shape
    if order is None:
        order = lax.broadcasted_iota(I32, (n, t), 0)
    slot = lax.broadcasted_iota(I32, (k, t), 0)
    big = jnp.iinfo(jnp.int32).max

    def step(i, carry):
        x, vals, picks = carry
        m = jnp.max(x, axis=0, keepdims=True)
        first = jnp.min(jnp.where(x == m, order, big), axis=0, keepdims=True)
        hit = order == first
        pick = first if payload is None else jnp.max(jnp.where(hit, payload, -1), axis=0, keepdims=True)
        vals = jnp.where(slot == i, m, vals)
        picks = jnp.where(slot == i, pick, picks)
        return jnp.where(hit, NEG_INF, x), vals, picks

    _, vals, picks = lax.fori_loop(0, k, step, (x, jnp.zeros((k, t), F32), jnp.zeros((k, t), I32)), unroll=4)
    return vals, picks


def _peer_route_kernel(qp_ref, keys_ref, pos_ref, pad_ref, e_ref, g_ref, sv_ref, si_ref):
    tm = qp_ref.shape[0]

    def head(h, carry):
        s = []
        for p in range(2):
            c0 = pl.multiple_of((h * 2 + p) * HALF_KEY, HALF_KEY)
            q_hp = qp_ref[:, pl.ds(c0, HALF_KEY)].astype(BF16)
            s.append(lax.dot_general(keys_ref[p], q_hp, NT_DIMS, preferred_element_type=F32))
        sv_ref[h], si_ref[h] = _topk_rows(jnp.concatenate(s, axis=1), TOPK)
        return carry

    lax.fori_loop(0, PEER_HEADS, head, 0)
    for h0 in range(0, PEER_HEADS, ROUTE_GROUP):
        heads = range(h0, h0 + ROUTE_GROUP)
        side = lambda ref, p: jnp.concatenate([ref[h][:, p * tm:(p + 1) * tm] for h in heads], axis=1)
        comb = _pair_combine(side(sv_ref, 0), side(sv_ref, 1), lambda x, y: x + y) + pad_ref[...]
        cidx = _pair_combine(side(si_ref, 0), side(si_ref, 1), lambda x, y: x * N_KEYS + y)
        tv, eidx = _topk_rows(comb, TOPK, order=pos_ref[...], payload=cidx)
        ex = jnp.exp(tv - jnp.max(tv, axis=0, keepdims=True))
        gate = ex / jnp.sum(ex, axis=0, keepdims=True)
        for n, h in enumerate(heads):
            e_ref[h * TOPK:(h + 1) * TOPK, :] = eidx[:, n * tm:(n + 1) * tm] * HALF_ROWS + HALF_ROWS
            g_ref[h * TOPK:(h + 1) * TOPK, :] = gate[:, n * tm:(n + 1) * tm]


def _peer_route(qp, keys):
    t = qp.shape[0]
    tm = V7X_LANES
    rows = PEER_HEADS * TOPK
    tables = _candidate_tables(ROUTE_GROUP * tm)
    out = pl.BlockSpec((rows, tm), lambda i: (0, i))
    return pl.pallas_call(
        _peer_route_kernel,
        grid=(t // tm,),
        in_specs=[pl.BlockSpec((tm, qp.shape[1]), lambda i: (i, 0)), _const_spec(keys.shape)]
                 + [_const_spec(a.shape) for a in tables],
        out_specs=(out, out),
        out_shape=(jax.ShapeDtypeStruct((rows, t), I32), jax.ShapeDtypeStruct((rows, t), F32)),
        scratch_shapes=[pltpu.VMEM((PEER_HEADS, TOPK, 2 * tm), F32), pltpu.VMEM((PEER_HEADS, TOPK, 2 * tm), I32)],
        compiler_params=_cparams("arbitrary"),
        name="peer_route",
    )(qp, keys, *tables)


PAIRS = PEER_HEADS * TOPK
HALF_ROWS = D_MODEL // 2 // V7X_LANES
HI_MASK = -65536


def _pack_expert_table(tab):
    bits = lax.bitcast_convert_type(tab.astype(BF16), jnp.uint16).astype(jnp.uint32)
    word = bits[:, :D_MODEL // 2] | (bits[:, D_MODEL // 2:] << 16)
    flat = lax.bitcast_convert_type(word, I32).reshape(tab.shape[0] * HALF_ROWS, V7X_LANES)
    return jnp.pad(flat, ((HALF_ROWS, V7X_SUBLANES - HALF_ROWS), (0, 0)))


def _expert_halves(tab_ref, off):
    w = tab_ref[pl.ds(off, V7X_SUBLANES), :]
    return pltpu.bitcast(w << 16, F32), pltpu.bitcast(w & HI_MASK, F32)


def _row_halves(x, upper):
    pad = jnp.zeros((V7X_SUBLANES - HALF_ROWS, V7X_LANES), F32)
    chunk = lambda c: x[:, c * V7X_LANES:(c + 1) * V7X_LANES]
    tile = lambda rows: jnp.concatenate([pad] + rows if upper else rows + [pad], axis=0)
    return (tile([chunk(r) for r in range(HALF_ROWS)]), tile([chunk(HALF_ROWS + r) for r in range(HALF_ROWS)]))


def _pair_row_sums(ps):
    sub = lax.broadcasted_iota(I32, ps[0].shape, 0)
    level = [ps[0] + ps[4], ps[2] + ps[6], ps[1] + ps[5], ps[3] + ps[7]]
    for h in (2, 1):
        first = (sub & (2 * h - 1)) < h
        level = [jnp.where(first, x, pltpu.roll(y, h, 0)) + jnp.where(first, pltpu.roll(x, V7X_SUBLANES - h, 0), y)
                 for x, y in zip(level[0::2], level[1::2])]
    return level[0]


def _peer_dot_kernel(off_ref, x_ref, g_ref, tab_ref, w_ref, r_ref, a_ref, *, tb):
    def token(t, carry):
        x_row = x_ref[pl.ds(t, 1), :]
        x_low, x_up = _row_halves(x_row, False), _row_halves(x_row, True)
        offs = off_ref.at[t]
        base = pl.multiple_of(t * PAIRS, PAIRS)
        for grp in range(PAIRS // V7X_SUBLANES):
            ps = []
            for j in range(V7X_SUBLANES):
                upper = j >= HALF_ROWS
                off = offs[j * TOPK + grp]
                lo, hi = _expert_halves(tab_ref, off - HALF_ROWS if upper else off)
                x_lo, x_hi = x_up if upper else x_low
                ps.append(lo * x_lo + hi * x_hi)
            r_ref[pl.ds(base + grp * V7X_SUBLANES, V7X_SUBLANES), :] = _pair_row_sums(ps)
        return carry

    lax.fori_loop(0, tb, token, 0)
    ones = jnp.ones((V7X_SUBLANES, V7X_LANES), BF16)
    r = r_ref[...]
    hi = r.astype(BF16)
    lo = (r - hi.astype(F32)).astype(BF16)
    sums = (lax.dot_general(ones, hi, NT_DIMS, preferred_element_type=F32)
            + lax.dot_general(ones, lo, NT_DIMS, preferred_element_type=F32))
    for t in range(tb):
        a_ref[t:t + 1, :] = sums[0:1, t * PAIRS:(t + 1) * PAIRS]
    a = a_ref[...]
    gelu = a * (lax.erf(a * (2.0 ** -0.5)) + 1.0) * 0.5
    w_ref[...] = g_ref[...] * gelu


def _peer_dot(off, x, g, tab, *, tb):
    t = off.shape[0]
    return pl.pallas_call(
        functools.partial(_peer_dot_kernel, tb=tb),
        grid=(t // tb,),
        in_specs=[pl.BlockSpec((tb, PAIRS), lambda i: (i, 0), memory_space=pltpu.SMEM),
                  pl.BlockSpec((tb, D_MODEL), lambda i: (i, 0)),
                  pl.BlockSpec((tb, PAIRS), lambda i: (i, 0)),
                  _const_spec(tab.shape)],
        out_specs=pl.BlockSpec((tb, PAIRS), lambda i: (i, 0)),
        out_shape=jax.ShapeDtypeStruct((t, PAIRS), F32),
        scratch_shapes=[pltpu.VMEM((tb * PAIRS, V7X_LANES), F32), pltpu.VMEM((tb, PAIRS), F32)],
        compiler_params=_cparams("arbitrary"),
        name="peer_dot",
    )(off, x, g, tab)


def _peer_sum_kernel(off_ref, w_ref, tab_ref, y_ref, wb_ref, *, tb):
    n_acc = 2

    def spread(t):
        return jnp.broadcast_to(w_ref[pl.ds(t, 1), :], (PAIRS, PAIRS)).T

    wb_ref[...] = spread(0)

    def token_group(g, carry):
        t0 = pl.multiple_of(g * V7X_SUBLANES, V7X_SUBLANES)
        y_rows = y_ref.at[pl.ds(t0, V7X_SUBLANES), :]
        for j in range(V7X_SUBLANES):
            t = t0 + j
            zero = jnp.zeros((V7X_SUBLANES, V7X_LANES), F32)
            acc_lo, acc_hi = [zero] * n_acc, [zero] * n_acc
            offs = [off_ref.at[t, pl.ds(c * TOPK, TOPK)] for c in range(PEER_HEADS)]
            nxt = spread(jnp.minimum(t + 1, tb - 1))
            for n in range(PAIRS):
                lo, hi = _expert_halves(tab_ref, offs[n % PEER_HEADS][n // PEER_HEADS])
                wk = wb_ref[n:n + 1, :]
                acc_lo[n % n_acc] = acc_lo[n % n_acc] + wk * lo
                acc_hi[n % n_acc] = acc_hi[n % n_acc] + wk * hi
            for half, acc in enumerate((acc_lo[0] + acc_lo[1], acc_hi[0] + acc_hi[1])):
                for r in range(HALF_ROWS):
                    c = (half * HALF_ROWS + r) * V7X_LANES
                    y_rows[j:j + 1, c:c + V7X_LANES] = acc[r:r + 1, :]
            wb_ref[...] = nxt
        return carry

    lax.fori_loop(0, tb // V7X_SUBLANES, token_group, 0)


def _peer_sum(off, w, tab, *, tb):
    t = off.shape[0]
    return pl.pallas_call(
        functools.partial(_peer_sum_kernel, tb=tb),
        grid=(t // tb,),
        in_specs=[pl.BlockSpec((tb, PAIRS), lambda i: (i, 0), memory_space=pltpu.SMEM),
                  pl.BlockSpec((tb, PAIRS), lambda i: (i, 0)), _const_spec(tab.shape)],
        out_specs=pl.BlockSpec((tb, D_MODEL), lambda i: (i, 0)),
        out_shape=jax.ShapeDtypeStruct((t, D_MODEL), F32),
        scratch_shapes=[pltpu.VMEM((PAIRS, PAIRS), F32)],
        compiler_params=_cparams("arbitrary"),
        name="peer_sum",
    )(off, w, tab)


def _ple_kernel(h_ref, y_ref, p_ref, gple_ref, wple_ref, wgate_ref, gfin_ref, o_ref):
    h = h_ref[...] + y_ref[...]
    gate = jax.nn.sigmoid(jnp.dot(_rms(h, gple_ref[...]).astype(BF16), wgate_ref[...], preferred_element_type=F32))
    h = h + jnp.dot(p_ref[...].astype(BF16), wple_ref[...], preferred_element_type=F32) * gate
    o_ref[...] = _rms(h, gfin_ref[...])


def _ple(h, y, p, g_ple, wple, wgate, g_final, *, tm):
    t = h.shape[0]
    row = lambda w: pl.BlockSpec((tm, w), lambda i: (i, 0))
    return pl.pallas_call(
        _ple_kernel,
        grid=(t // tm,),
        in_specs=[row(D_MODEL), row(D_MODEL), row(p.shape[1]), _const_spec(g_ple.shape),
                  _const_spec(wple.shape), _const_spec(wgate.shape), _const_spec(g_final.shape)],
        out_specs=row(D_MODEL),
        out_shape=jax.ShapeDtypeStruct((t, D_MODEL), F32),
        compiler_params=_cparams("arbitrary"),
        name="ple_out",
    )(h, y, p, g_ple, wple, wgate, g_final)


def _trunk(x, o, ydw, p, wts, *, tm, tb):
    h, xb, qp = _mix(x, o, ydw, wts["g_mix"], wts["wgate"], wts["watt"], wts["ln_g"], wts["ln_b"],
                     wts["wconv"], wts["wout"], wts["g_ffn"], wts["wpq"], tm=tm)
    e_t, g_t = _peer_route(qp, wts["keys"])
    off = e_t.T
    g_visit = g_t.reshape(PEER_HEADS, TOPK, -1).transpose(2, 1, 0).reshape(-1, PAIRS)
    w = _peer_dot(off, xb, g_visit, wts["tab_u"], tb=tb)
    y = _peer_sum(off, w, wts["tab_v"], tb=2 * tb)
    return _ple(h, y, p, wts["g_ple"], wts["wple"], wts["wplegate"], wts["g_final"], tm=tm)


def kernel(x_prompt, x_sample, cache_k, cache_v, cache_lf, state_conv, page_table, p_prompt, p_sample,
           g_mix, w_in, b_f, w_dw, b_dw, ln_g, ln_b, w_conv_proj, w_att_proj, w_out, g_ffn, w_pq,
           sub_keys, exp_u, exp_v, g_ple, w_ple, w_ple_gate, g_final):
    assert g_mix.shape[0] == 1, "single-layer trunk"
    b, s, _ = x_prompt.shape
    bd, sd, _ = x_sample.shape
    tp, ts = b * s, bd * sd
    page = cache_k.shape[2]

    w = w_in[0]
    c_f, c_glu, c_gate = 3 * ATT_WIDTH, 3 * ATT_WIDTH + N_HEADS, 3 * ATT_WIDTH + N_HEADS + 2 * CONV_CH
    wqkv = w[:, :c_f].astype(BF16)
    wf_t = w[:, c_f:c_glu].T.astype(BF16)
    wglu = w[:, c_glu:c_gate].astype(BF16)
    row = lambda a: a.reshape(1, -1)
    w_dw_pad = jnp.pad(w_dw[0], ((0, 1), (0, 0)))
    wts = dict(
        g_mix=row(g_mix[0]), wgate=w[:, c_gate:].astype(BF16), watt=w_att_proj[0].astype(BF16),
        ln_g=row(ln_g[0]), ln_b=row(ln_b[0]), wconv=w_conv_proj[0].astype(BF16), wout=w_out[0].astype(BF16),
        g_ffn=row(g_ffn[0]), wpq=w_pq[0].astype(BF16), keys=sub_keys[0].astype(BF16),
        tab_u=_pack_expert_table(exp_u[0]), tab_v=_pack_expert_table(exp_v[0]),
        g_ple=row(g_ple[0]), wple=w_ple[0].astype(BF16), wplegate=w_ple_gate[0].astype(BF16),
        g_final=row(g_final))
    b_f_col = b_f[0].reshape(N_HEADS, 1)
    b_dw_row = row(b_dw[0])

    xp = x_prompt.reshape(tp, D_MODEL)
    q, k, v, kb, vb, lf, cum, glu = _in_projection(xp, wts["g_mix"], wqkv, wf_t, b_f_col, wglu, seg=s, tm=512,
                                                   q_scale=LOG2E * HEAD_DIM ** -0.5)
    o = _fox_prompt(q, kb, vb, cum.reshape(N_HEADS // 2, 2, tp), batch=b, seq=s, tq=512)
    ydw = _dwconv_prompt(glu, w_dw_pad, b_dw_row, seq=s, tm=512)
    y_prompt = _trunk(xp, o, ydw, p_prompt[0].reshape(tp, -1), wts, tm=512, tb=64).reshape(b, s, D_MODEL)
    k_prompt = k.reshape(1, b, s, N_HEADS, HEAD_DIM)
    v_prompt = v.reshape(1, b, s, N_HEADS, HEAD_DIM)
    lf_prompt = lf.T.reshape(1, b, s, N_HEADS)
    conv_prompt = glu.reshape(b, s, CONV_CH)[None, :, s - (CONV_WIDTH - 1):, :]

    xs = x_sample.reshape(ts, D_MODEL)
    q, k, v, kb, vb, lf, cum, glu = _in_projection(xs, wts["g_mix"], wqkv, wf_t, b_f_col, wglu, seg=sd, tm=ts,
                                                   q_scale=HEAD_DIM ** -0.5)
    page_t = lambda a: jnp.swapaxes(jnp.pad(a.reshape(bd, sd, ATT_WIDTH), ((0, 0), (0, page - sd), (0, 0))), 1, 2)
    cn = jnp.pad(cum.reshape(N_HEADS, bd, sd).transpose(1, 0, 2), ((0, 0), (0, 0), (0, page - sd)))
    o = _fox_sample(q.astype(F32).reshape(bd, sd, ATT_WIDTH), page_t(kb), page_t(vb), cn,
                    jnp.transpose(cache_k[0], (0, 2, 3, 1)), jnp.transpose(cache_v[0], (0, 2, 3, 1)),
                    jnp.swapaxes(cache_lf[0], 1, 2), page_table, pp=16).reshape(ts, ATT_WIDTH)
    glu_s = glu.reshape(bd, sd, CONV_CH)
    ydw = _dwconv_sample(jnp.swapaxes(state_conv[0], 0, 1), jnp.swapaxes(glu_s, 0, 1), w_dw_pad, b_dw_row)
    ydw = jnp.swapaxes(ydw, 0, 1).reshape(ts, CONV_CH)
    y_sample = _trunk(xs, o, ydw, p_sample[0].reshape(ts, -1), wts, tm=ts, tb=64).reshape(bd, sd, D_MODEL)
    k_sample = k.reshape(1, bd, sd, N_HEADS, HEAD_DIM)
    v_sample = v.reshape(1, bd, sd, N_HEADS, HEAD_DIM)
    lf_sample = lf.T.reshape(1, bd, sd, N_HEADS)
    conv_sample = jnp.concatenate([state_conv[0][:, sd:], glu_s], axis=1)[None]

    return (y_prompt, y_sample, k_prompt, v_prompt, lf_prompt, conv_prompt,
            k_sample, v_sample, lf_sample, conv_sample)
```
